```python
import math
import jax, jax.numpy as jnp
from jax import lax
import numpy as np

D_MODEL = 1024
BATCH = 8
SEQ = 2048
DEPTH = 1
DEC_BATCH = 8
DEC_SEQ = 8192
PAST_LEN = 128

N_MEM = 256
EPS = 1e-6
H_M = 4
DK_M = 128
DV_M = 256
CHUNK = 64
W_M = H_M * DV_M
H_D = 8
DK_D = 64
DV_D = 128
W_D = H_D * DV_D
ROT_DIM = DK_D // 4
ROPE_THETA = 500000.0
Q_BLOCK = 128
H_X = 4
DH_X = D_MODEL // H_X
D_FF = 2816

SPLIT_SIZES = (H_M * DK_M, H_M * DK_M, W_M, W_M, 2 * H_M, 2 * H_M,
               H_D * 2 * DK_D, H_D * 2 * DK_D, W_D, D_MODEL, D_MODEL)
N_MIX_COLS = sum(SPLIT_SIZES)
SPLIT_OFFSETS = tuple(int(o) for o in np.cumsum(SPLIT_SIZES)[:-1])

kernel_name = "hybrid_mlstm_diffattn_encoder"


def rms_norm(x, g):
    xf = x.astype(jnp.float32)
    y = xf * lax.rsqrt(jnp.mean(xf * xf, axis=-1, keepdims=True) + EPS)
    return (y * g.astype(jnp.float32)).astype(x.dtype)


def swiglu_ffn(x, w_in, w_out):
    g, u = jnp.split(x @ w_in, 2, axis=-1)
    return (jax.nn.silu(g) * u) @ w_out


def partial_rope(x, pos):
    inv_freq = ROPE_THETA ** (-jnp.arange(0, ROT_DIM, 2, dtype=jnp.float32) / ROT_DIM)
    ang = pos[:, None] * inv_freq[None, :]
    cos = jnp.concatenate([jnp.cos(ang)] * 2, axis=-1)[None, :, None, None, :]
    sin = jnp.concatenate([jnp.sin(ang)] * 2, axis=-1)[None, :, None, None, :]
    xr = x[..., :ROT_DIM].astype(jnp.float32)
    x1, x2 = jnp.split(xr, 2, axis=-1)
    xr = xr * cos + jnp.concatenate([-x2, x1], axis=-1) * sin
    return jnp.concatenate([xr.astype(x.dtype), x[..., ROT_DIM:]], axis=-1)


def mlstm_chunkwise(q, k, v, ig, lf):
    B, H, S, DK = q.shape
    DV = v.shape[-1]
    nc = S // CHUNK

    def to_chunks(t):
        return jnp.moveaxis(t.reshape(B, H, nc, CHUNK, *t.shape[3:]), 2, 0)

    qc, kc, vc, igc, lfc = (to_chunks(t) for t in (q, k, v, ig, lf))
    lower = jnp.tril(jnp.ones((CHUNK, CHUNK), dtype=bool))

    def step(carry, inp):
        C, n, m = carry
        qj, kj, vj, igj, lfj = inp
        b = jnp.cumsum(lfj, axis=-1)
        dmat = jnp.where(lower, b[..., :, None] - b[..., None, :] + igj[..., None, :], -jnp.inf)
        inter = b + m[..., None]
        m_row = jnp.maximum(inter, jnp.max(dmat, axis=-1))
        w = jnp.exp(dmat - m_row[..., None]) * jnp.einsum('bhid,bhjd->bhij', qj, kj)
        s_inter = jnp.exp(inter - m_row)
        num = s_inter[..., None] * jnp.einsum('bhvd,bhid->bhiv', C, qj) + jnp.einsum('bhij,bhjv->bhiv', w, vj)
        den = s_inter * jnp.einsum('bhd,bhid->bhi', n, qj) + jnp.sum(w, axis=-1)
        h = num / jnp.maximum(jnp.abs(den), jnp.exp(-m_row))[..., None]
        b_last = b[..., -1]
        a = b_last[..., None] - b + igj
        m_new = jnp.maximum(b_last + m, jnp.max(a, axis=-1))
        decay = jnp.exp(b_last + m - m_new)
        wa = jnp.exp(a - m_new[..., None])
        C_new = decay[..., None, None] * C + jnp.einsum('bhj,bhjv,bhjd->bhvd', wa, vj, kj)
        n_new = decay[..., None] * n + jnp.einsum('bhj,bhjd->bhd', wa, kj)
        return (C_new, n_new, m_new), h

    init = (jnp.zeros((B, H, DV, DK), jnp.float32), jnp.zeros((B, H, DK), jnp.float32),
            jnp.zeros((B, H), jnp.float32))
    _, hs = lax.scan(step, init, (qc, kc, vc, igc, lfc))
    return jnp.moveaxis(hs, 0, 2).reshape(B, H, S, DV)


def mlstm_bidirectional(q, k, v, ig, lf):
    h_f = mlstm_chunkwise(q, k, v, ig[..., 0], lf[..., 0])
    flip = lambda t: jnp.flip(t, axis=2)
    h_b = flip(mlstm_chunkwise(flip(q), flip(k), flip(v), flip(ig[..., 1]), flip(lf[..., 1])))
    return h_f + h_b


def diff_attention(q, k, v, lam):
    B, H, _, S, DK = q.shape
    nqb = S // Q_BLOCK
    qb = jnp.moveaxis(q.reshape(B, H, 2, nqb, Q_BLOCK, DK), 3, 0)
    scale = DK ** -0.5

    def block(qi):
        s = jnp.einsum('bhcqd,bhckd->bhcqk', qi, k).astype(jnp.float32) * scale
        p = jax.nn.softmax(s, axis=-1)
        a = p[:, :, 0] - lam * p[:, :, 1]
        return jnp.einsum('bhqk,bhkv->bhqv', a.astype(v.dtype), v)

    o = lax.map(block, qb)
    return jnp.moveaxis(o, 0, 2).reshape(B, H, S, v.shape[-1])


def token_mixing(u, w_mix_in, b_igate, b_fgate, mlstm_norm, w_branch_a, lambda_q1, lambda_k1,
                 lambda_q2, lambda_k2, diff_norm, w_branch_b, w_mix_out, lambda_init):
    B, S, _ = u.shape
    z = u @ w_mix_in
    (q_m, k_m, v_m, o_m, ig, fg, q_d, k_d, v_d, g_a, g_b) = jnp.split(z, list(SPLIT_OFFSETS), axis=-1)

    def heads(t, h):
        return t.reshape(B, S, h, -1).transpose(0, 2, 1, 3)

    qm = heads(q_m, H_M).astype(jnp.float32)
    km = heads(k_m, H_M).astype(jnp.float32) * (DK_M ** -0.5)
    vm = heads(v_m, H_M).astype(jnp.float32)
    igp = (ig.reshape(B, S, 2, H_M).astype(jnp.float32) + b_igate.astype(jnp.float32)).transpose(0, 3, 1, 2)
    lfp = jax.nn.log_sigmoid(fg.reshape(B, S, 2, H_M).astype(jnp.float32)
                             + b_fgate.astype(jnp.float32)).transpose(0, 3, 1, 2)
    h_m = mlstm_bidirectional(qm, km, vm, igp, lfp).transpose(0, 2, 1, 3).astype(u.dtype)
    h_m = rms_norm(h_m, mlstm_norm.reshape(H_M, DV_M)).reshape(B, S, W_M) * jax.nn.sigmoid(o_m)
    y_a = h_m @ w_branch_a

    pos = jnp.arange(S, dtype=jnp.float32)
    qd = partial_rope(q_d.reshape(B, S, H_D, 2, DK_D), pos).transpose(0, 2, 3, 1, 4)
    kd = partial_rope(k_d.reshape(B, S, H_D, 2, DK_D), pos).transpose(0, 2, 3, 1, 4)
    vd = heads(v_d, H_D)
    lam = (jnp.exp(jnp.sum(lambda_q1.astype(jnp.float32) * lambda_k1.astype(jnp.float32)))
           - jnp.exp(jnp.sum(lambda_q2.astype(jnp.float32) * lambda_k2.astype(jnp.float32))) + lambda_init)
    o_d = diff_attention(qd, kd, vd, lam).transpose(0, 2, 1, 3)
    o_d = rms_norm(o_d, diff_norm.reshape(H_D, DV_D)) * (1.0 - lambda_init)
    y_b = o_d.reshape(B, S, W_D) @ w_branch_b

    merged = jax.nn.sigmoid(g_a) * y_a + jax.nn.sigmoid(g_b) * y_b
    return merged @ w_mix_out


def memory_cross_attention(u, mem_n, w_xq, w_xkv, w_xo):
    B, S, _ = u.shape
    M = mem_n.shape[1]
    q = (u @ w_xq).reshape(B, S, H_X, DH_X)
    k, v = jnp.split(mem_n @ w_xkv, 2, axis=-1)
    k = k.reshape(B, M, H_X, DH_X)
    v = v.reshape(B, M, H_X, DH_X)
    s = jnp.einsum('bshd,bmhd->bhsm', q, k).astype(jnp.float32) * (DH_X ** -0.5)
    p = jax.nn.softmax(s, axis=-1)
    o = jnp.einsum('bhsm,bmhd->bshd', p.astype(v.dtype), v).reshape(B, S, D_MODEL)
    return o @ w_xo


def encoder_layer(x, mem, layer_idx, ffn1_norm, ffn1_w_in, ffn1_w_out, mix_norm, w_mix_in, b_igate, b_fgate,
                  mlstm_norm, w_branch_a, lambda_q1, lambda_k1, lambda_q2, lambda_k2, diff_norm, w_branch_b,
                  w_mix_out, xattn_norm, mem_norm, w_xq, w_xkv, w_xo, ffn2_norm, ffn2_w_in, ffn2_w_out):
    x = x + 0.5 * swiglu_ffn(rms_norm(x, ffn1_norm), ffn1_w_in, ffn1_w_out)
    lambda_init = 0.8 - 0.6 * math.exp(-0.3 * layer_idx)
    x = x + token_mixing(rms_norm(x, mix_norm), w_mix_in, b_igate, b_fgate, mlstm_norm, w_branch_a,
                         lambda_q1, lambda_k1, lambda_q2, lambda_k2, diff_norm, w_branch_b, w_mix_out, lambda_init)
    x = x + memory_cross_attention(rms_norm(x, xattn_norm), rms_norm(mem, mem_norm), w_xq, w_xkv, w_xo)
    x = x + 0.5 * swiglu_ffn(rms_norm(x, ffn2_norm), ffn2_w_in, ffn2_w_out)
    return x


def trunk(x, mem, layer_weights, final_norm):
    for l in range(DEPTH):
        x = encoder_layer(x, mem, l, *[w[l] for w in layer_weights])
    return rms_norm(x, final_norm)


def setup_inputs(seed: int = 0) -> dict:
    key = jax.random.key(seed)
    ks = jax.random.split(key, 32)
    f32 = jnp.float32
    L = DEPTH

    def normal(k, shape, scale):
        return jax.random.normal(k, shape, f32) * scale

    def gain(k, shape):
        return 1.0 + 0.05 * jax.random.normal(k, shape, f32)

    return {
        "x_prompt": normal(ks[0], (BATCH, SEQ, D_MODEL), 1.0),
        "x_sample": normal(ks[1], (DEC_BATCH, DEC_SEQ, D_MODEL), 1.0),
        "mem_prompt": normal(ks[2], (BATCH, N_MEM, D_MODEL), 1.0),
        "mem_sample": normal(ks[3], (DEC_BATCH, N_MEM, D_MODEL), 1.0),
        "ffn1_norm": gain(ks[4], (L, D_MODEL)),
        "ffn1_w_in": normal(ks[5], (L, D_MODEL, 2 * D_FF), D_MODEL ** -0.5),
        "ffn1_w_out": normal(ks[6], (L, D_FF, D_MODEL), D_FF ** -0.5),
        "mix_norm": gain(ks[7], (L, D_MODEL)),
        "w_mix_in": normal(ks[8], (L, D_MODEL, N_MIX_COLS), D_MODEL ** -0.5),
        "b_igate": normal(ks[9], (L, 2, H_M), 0.1),
        "b_fgate": jnp.linspace(3.0, 6.0, H_M, dtype=f32) + normal(ks[10], (L, 2, H_M), 0.1),
        "mlstm_norm": gain(ks[11], (L, W_M)),
        "w_branch_a": normal(ks[12], (L, W_M, D_MODEL), W_M ** -0.5),
        "lambda_q1": normal(ks[13], (L, DK_D), 0.1),
        "lambda_k1": normal(ks[14], (L, DK_D), 0.1),
        "lambda_q2": normal(ks[15], (L, DK_D), 0.1),
        "lambda_k2": normal(ks[16], (L, DK_D), 0.1),
        "diff_norm": gain(ks[17], (L, W_D)),
        "w_branch_b": normal(ks[18], (L, W_D, D_MODEL), W_D ** -0.5),
        "w_mix_out": normal(ks[19], (L, D_MODEL, D_MODEL), D_MODEL ** -0.5),
        "xattn_norm": gain(ks[20], (L, D_MODEL)),
        "mem_norm": gain(ks[21], (L, D_MODEL)),
        "w_xq": normal(ks[22], (L, D_MODEL, D_MODEL), D_MODEL ** -0.5),
        "w_xkv": normal(ks[23], (L, D_MODEL, 2 * D_MODEL), D_MODEL ** -0.5),
        "w_xo": normal(ks[24], (L, D_MODEL, D_MODEL), D_MODEL ** -0.5),
        "ffn2_norm": gain(ks[25], (L, D_MODEL)),
        "ffn2_w_in": normal(ks[26], (L, D_MODEL, 2 * D_FF), D_MODEL ** -0.5),
        "ffn2_w_out": normal(ks[27], (L, D_FF, D_MODEL), D_FF ** -0.5),
        "final_norm": gain(ks[28], (D_MODEL,)),
    }


def reference(x_prompt, x_sample, mem_prompt, mem_sample, ffn1_norm, ffn1_w_in, ffn1_w_out, mix_norm,
              w_mix_in, b_igate, b_fgate, mlstm_norm, w_branch_a, lambda_q1, lambda_k1, lambda_q2, lambda_k2,
              diff_norm, w_branch_b, w_mix_out, xattn_norm, mem_norm, w_xq, w_xkv, w_xo, ffn2_norm,
              ffn2_w_in, ffn2_w_out, final_norm):
    layer_weights = (ffn1_norm, ffn1_w_in, ffn1_w_out, mix_norm, w_mix_in, b_igate, b_fgate, mlstm_norm,
                     w_branch_a, lambda_q1, lambda_k1, lambda_q2, lambda_k2, diff_norm, w_branch_b, w_mix_out,
                     xattn_norm, mem_norm, w_xq, w_xkv, w_xo, ffn2_norm, ffn2_w_in, ffn2_w_out)
    y_prompt = trunk(x_prompt, mem_prompt, layer_weights, final_norm)
    y_sample = trunk(x_sample, mem_sample, layer_weights, final_norm)
    return (y_prompt, y_sample)
```

```python
import functools
import math

import jax
import jax.numpy as jnp
from jax import lax
from jax.experimental import pallas as pl
from jax.experimental.pallas import tpu as pltpu

F32 = jnp.float32
BF16 = jnp.bfloat16

D_MODEL = 1024
EPS = 1e-6
H_M, DK_M, DV_M = 4, 128, 256
W_M = H_M * DV_M
H_D, DK_D, DV_D = 8, 64, 128
W_D = H_D * DV_D
ROT_DIM = DK_D // 4
ROPE_THETA = 500000.0
H_X = 4
DH_X = D_MODEL // H_X
D_FF = 2816
SPLIT_SIZES = (H_M * DK_M, H_M * DK_M, W_M, W_M, 2 * H_M, 2 * H_M,
               H_D * 2 * DK_D, H_D * 2 * DK_D, W_D, D_MODEL, D_MODEL)

V7X_LANES = 128
V7X_VMEM_LIMIT_BYTES = 56 * 1024 * 1024

TOKEN_TILE = 256
MLSTM_CHUNK = 256
ATTN_TQ = 256
ATTN_TK = 512
ONES_ROWS = 16
VT_ROWS = DV_D + ONES_ROWS


def _params(*sem):
    return pltpu.CompilerParams(dimension_semantics=sem, vmem_limit_bytes=V7X_VMEM_LIMIT_BYTES)


def _const_spec(shape):
    n = len(shape)
    return pl.BlockSpec(shape, lambda *_: (0,) * n, pipeline_mode=pl.Buffered(1))


def _rms(x, g):
    return x * lax.rsqrt(jnp.mean(x * x, axis=-1, keepdims=True) + EPS) * g


def _sigmoid(x):
    return 1.0 / (1.0 + jnp.exp(-x))


_NT = (((1,), (1,)), ((), ()))
_TN = (((0,), (0,)), ((), ()))


def _ffn_kernel(x_ref, g_ref, win_ref, wout_ref, fin_ref, o_ref, *, final_norm):
    x = x_ref[...]
    u = _rms(x, g_ref[...]).astype(BF16)
    z = jnp.dot(u, win_ref[...], preferred_element_type=F32)
    gate, up = z[:, :D_FF], z[:, D_FF:]
    h = (gate * _sigmoid(gate) * up).astype(BF16)
    y = x + 0.5 * jnp.dot(h, wout_ref[...], preferred_element_type=F32)
    if final_norm:
        y = _rms(y, fin_ref[...])
    o_ref[...] = y


def _ffn(x2d, g, w_in, w_out, fin, final_norm):
    t = x2d.shape[0]
    tm = TOKEN_TILE
    row = pl.BlockSpec((tm, D_MODEL), lambda i: (i, 0))
    return pl.pallas_call(
        functools.partial(_ffn_kernel, final_norm=final_norm),
        grid=(t // tm,),
        in_specs=[row, _const_spec((1, D_MODEL)), _const_spec((D_MODEL, 2 * D_FF)),
                  _const_spec((D_FF, D_MODEL)), _const_spec((1, D_MODEL))],
        out_specs=row,
        out_shape=jax.ShapeDtypeStruct((t, D_MODEL), F32),
        compiler_params=_params("parallel"),
        name="ffn_final" if final_norm else "ffn",
    )(x2d, g, w_in, w_out, fin)


def _mix_proj_kernel(x_ref, g_ref, cos_ref, sa_ref, sb_ref, wqm, wkm, wvm, wom, wqd, wkd, wvdt, wga, wgb,
                     wgt, bg_ref, qm_o, km_o, vm_o, om_o, qd_o, kd_o, vdt_o, ga_o, gb_o, gt_o):
    u = _rms(x_ref[0], g_ref[...]).astype(BF16)
    tm = u.shape[0]

    def mm(w):
        return jnp.dot(u, w[...], preferred_element_type=F32)

    qm_o[0] = mm(wqm).astype(BF16)
    km_o[0] = (mm(wkm) * (DK_M ** -0.5)).astype(BF16)
    vm_o[0] = mm(wvm).astype(BF16)
    om_o[0] = _sigmoid(mm(wom)).astype(BF16)
    ga_o[0] = _sigmoid(mm(wga)).astype(BF16)
    gb_o[0] = _sigmoid(mm(wgb)).astype(BF16)

    cos, sa, sb = cos_ref[...], sa_ref[...], sb_ref[...]

    def rope(z):
        heads = []
        for h in range(H_D):
            zh = z[:, h * V7X_LANES:(h + 1) * V7X_LANES]
            heads.append(zh * cos + pltpu.roll(zh, V7X_LANES - ROT_DIM // 2, 1) * sa
                         + pltpu.roll(zh, ROT_DIM // 2, 1) * sb)
        return jnp.concatenate(heads, axis=1)

    qd_o[0] = (rope(mm(wqd)) * (DK_D ** -0.5)).astype(BF16)
    kd_o[0] = rope(mm(wkd)).astype(BF16)

    vt = lax.dot_general(wvdt[...], u, _NT, preferred_element_type=F32)
    for h in range(H_D):
        vdt_o[0, h, 0, 0:DV_D, :] = vt[h * DV_D:(h + 1) * DV_D, :].astype(BF16)
        vdt_o[0, h, 0, DV_D:VT_ROWS, :] = jnp.ones((ONES_ROWS, tm), BF16)

    xg = lax.dot_general(wgt[...], u, _NT, preferred_element_type=F32) + bg_ref[...]
    row = lax.broadcasted_iota(jnp.int32, xg.shape, 0)
    log_sig = jnp.minimum(xg, 0.0) - jnp.log1p(jnp.exp(-jnp.abs(xg)))
    gates = jnp.where((row % 4) < 2, xg, log_sig)
    for h in range(H_M):
        gt_o[0, h, 0] = gates[h * 4:(h + 1) * 4, :]


def _mix_proj(x3d, g, tables, w):
    b, s, _ = x3d.shape
    tm = TOKEN_TILE
    assert tm == MLSTM_CHUNK and ATTN_TK % tm == 0
    per_tk = ATTN_TK // tm
    row = lambda n: pl.BlockSpec((1, tm, n), lambda bi, i: (bi, i, 0))
    tab = pl.BlockSpec((tm, V7X_LANES), lambda bi, i: (i, 0))
    bf = lambda n: jax.ShapeDtypeStruct((b, s, n), BF16)
    in_specs = [row(D_MODEL), _const_spec((1, D_MODEL)), tab, tab, tab,
                _const_spec((D_MODEL, H_M * DK_M)), _const_spec((D_MODEL, H_M * DK_M)),
                _const_spec((D_MODEL, W_M)), _const_spec((D_MODEL, W_M)),
                _const_spec((D_MODEL, 2 * H_D * DK_D)), _const_spec((D_MODEL, 2 * H_D * DK_D)),
                _const_spec((W_D, D_MODEL)), _const_spec((D_MODEL, D_MODEL)), _const_spec((D_MODEL, D_MODEL)),
                _const_spec((4 * H_M, D_MODEL)), _const_spec((4 * H_M, 1))]
    out_specs = [row(H_M * DK_M), row(H_M * DK_M), row(W_M), row(W_M), row(2 * H_D * DK_D), row(2 * H_D * DK_D),
                 pl.BlockSpec((1, H_D, 1, VT_ROWS, tm), lambda bi, i: (bi, 0, i // per_tk, 0, i % per_tk)),
                 row(D_MODEL), row(D_MODEL),
                 pl.BlockSpec((1, H_M, 1, 4, tm), lambda bi, i: (bi, 0, i, 0, 0))]
    out_shape = [bf(H_M * DK_M), bf(H_M * DK_M), bf(W_M), bf(W_M), bf(2 * H_D * DK_D), bf(2 * H_D * DK_D),
                 jax.ShapeDtypeStruct((b, H_D, s // ATTN_TK, VT_ROWS, ATTN_TK), BF16),
                 bf(D_MODEL), bf(D_MODEL),
                 jax.ShapeDtypeStruct((b, H_M, s // tm, 4, tm), F32)]
    return pl.pallas_call(
        _mix_proj_kernel, grid=(b, s // tm), in_specs=in_specs, out_specs=out_specs, out_shape=out_shape,
        compiler_params=_params("parallel", "parallel"), name="mix_proj",
    )(x3d, g, *tables, *w)


def _mlstm_chunk(q, k, v, ig_row, lf_row, state, reverse):
    c_t, n, m = state
    L = q.shape[0]
    ri = lax.broadcasted_iota(jnp.int32, (L, L), 0)
    ci = lax.broadcasted_iota(jnp.int32, (L, L), 1)
    tri = (ci >= ri) if reverse else (ci <= ri)
    eye = ci == ri
    b_col = jnp.sum(jnp.where(tri, lf_row, 0.0), axis=1, keepdims=True)
    b_row = jnp.sum(jnp.where(eye, b_col, 0.0), axis=0, keepdims=True)
    ig_col = jnp.sum(jnp.where(eye, ig_row, 0.0), axis=1, keepdims=True)
    b_last = jnp.sum(lf_row, axis=1, keepdims=True)

    dmat = jnp.where(tri, b_col - b_row + ig_row, -jnp.inf)
    inter = b_col + m
    m_row = jnp.maximum(inter, jnp.max(dmat, axis=1, keepdims=True))
    qk = lax.dot_general(q, k, _NT, preferred_element_type=F32)
    w = jnp.exp(dmat - m_row) * qk
    s_inter = jnp.exp(inter - m_row)
    num = (s_inter * jnp.dot(q, c_t.astype(BF16), preferred_element_type=F32)
           + jnp.dot(w.astype(BF16), v, preferred_element_type=F32))
    qn = jnp.sum(q.astype(F32) * n, axis=1, keepdims=True)
    den = s_inter * qn + jnp.sum(w, axis=1, keepdims=True)
    h = num / jnp.maximum(jnp.abs(den), jnp.exp(-m_row))

    a_row = b_last - b_row + ig_row
    m_new = jnp.maximum(b_last + m, jnp.max(a_row, axis=1, keepdims=True))
    decay = jnp.exp(b_last + m - m_new)
    wa_col = jnp.exp(b_last - b_col + ig_col - m_new)
    kw = k.astype(F32) * wa_col
    c_new = decay * c_t + lax.dot_general(kw.astype(BF16), v, _TN, preferred_element_type=F32)
    n_new = decay * n + jnp.sum(kw, axis=0, keepdims=True)
    return h, (c_new, n_new, m_new)


def _mlstm_kernel(q_ref, k_ref, v_ref, g_ref, o_ref, *, n_chunks):
    L = MLSTM_CHUNK
    init = (jnp.zeros((DK_M, DV_M), F32), jnp.zeros((1, DK_M), F32), jnp.zeros((1, 1), F32))

    def run(c, state, reverse):
        rows = pl.ds(pl.multiple_of(c * L, L), L)
        g = g_ref[0, 0, c]
        ig, lf = (g[1:2], g[3:4]) if reverse else (g[0:1], g[2:3])
        return _mlstm_chunk(q_ref[0, rows, :], k_ref[0, rows, :], v_ref[0, rows, :], ig, lf, state, reverse), rows

    def fwd(c, state):
        (h, state), rows = run(c, state, False)
        o_ref[0, rows, :] = h
        return state

    def bwd(i, state):
        (h, state), rows = run(n_chunks - 1 - i, state, True)
        o_ref[0, rows, :] += h
        return state

    lax.fori_loop(0, n_chunks, fwd, init)
    lax.fori_loop(0, n_chunks, bwd, init)


def _mlstm(qm, km, vm, gates):
    b, s, _ = qm.shape
    n_chunks = s // MLSTM_CHUNK
    return pl.pallas_call(
        functools.partial(_mlstm_kernel, n_chunks=n_chunks),
        grid=(b, H_M),
        in_specs=[pl.BlockSpec((1, s, DK_M), lambda bi, h: (bi, 0, h)),
                  pl.BlockSpec((1, s, DK_M), lambda bi, h: (bi, 0, h)),
                  pl.BlockSpec((1, s, DV_M), lambda bi, h: (bi, 0, h)),
                  pl.BlockSpec((1, 1, n_chunks, 4, MLSTM_CHUNK), lambda bi, h: (bi, h, 0, 0, 0))],
        out_specs=pl.BlockSpec((1, s, DV_M), lambda bi, h: (bi, 0, h)),
        out_shape=jax.ShapeDtypeStruct((b, s, W_M), F32),
        compiler_params=_params("parallel", "parallel"), name="mlstm",
    )(qm, km, vm, gates)


def _diff_attn_kernel(lam_ref, q_ref, k_ref, vt_ref, g_ref, o_ref, acc_ref, *, n_kv, lambda_init):
    tq, tk = ATTN_TQ, ATTN_TK
    q = q_ref[0]
    lane = lax.broadcasted_iota(jnp.int32, q.shape, 1)
    zero = jnp.zeros_like(q)
    q_bd = jnp.concatenate([jnp.where(lane < DK_D, q, zero), jnp.where(lane >= DK_D, q, zero)], axis=0)
    acc_ref[...] = jnp.zeros_like(acc_ref)

    def body(j, m):
        ks = k_ref[0, pl.ds(pl.multiple_of(j * tk, tk), tk), :]
        s = lax.dot_general(ks, q_bd, _NT, preferred_element_type=F32)
        m_new = jnp.maximum(m, jnp.max(s, axis=0, keepdims=True))
        alpha = jnp.exp(m - m_new)
        p = jnp.exp(s - m_new).astype(BF16)
        acc_ref[...] = acc_ref[...] * alpha + jnp.dot(vt_ref[0, 0, j], p, preferred_element_type=F32)
        return m_new

    lax.fori_loop(0, n_kv, body, jnp.full((1, 2 * tq), -jnp.inf, F32))

    lp = lam_ref[...]
    lam = (jnp.exp(jnp.sum(lp[0:1] * lp[1:2], axis=1, keepdims=True))
           - jnp.exp(jnp.sum(lp[2:3] * lp[3:4], axis=1, keepdims=True)) + lambda_init)
    acc = acc_ref[...]
    row_sum = acc[DV_D:DV_D + 1, :]
    o = acc[:DV_D, :tq] / row_sum[:, :tq] - lam * (acc[:DV_D, tq:] / row_sum[:, tq:])
    o = o * lax.rsqrt(jnp.mean(o * o, axis=0, keepdims=True) + EPS) * g_ref[0]
    o_ref[0] = (o * (1.0 - lambda_init)).T.astype(BF16)


def _diff_attn(lam_params, qd, kd, vdt, gain, lambda_init):
    b, s, _ = qd.shape
    n_kv = s // ATTN_TK
    return pl.pallas_call(
        functools.partial(_diff_attn_kernel, n_kv=n_kv, lambda_init=lambda_init),
        grid=(b, H_D, s // ATTN_TQ),
        in_specs=[_const_spec((4, DK_D)),
                  pl.BlockSpec((1, ATTN_TQ, 2 * DK_D), lambda bi, h, i: (bi, i, h)),
                  pl.BlockSpec((1, s, 2 * DK_D), lambda bi, h, i: (bi, 0, h)),
                  pl.BlockSpec((1, 1, n_kv, VT_ROWS, ATTN_TK), lambda bi, h, i: (bi, h, 0, 0, 0)),
                  pl.BlockSpec((1, DV_D, 1), lambda bi, h, i: (h, 0, 0))],
        out_specs=pl.BlockSpec((1, ATTN_TQ, DV_D), lambda bi, h, i: (bi, i, h)),
        out_shape=jax.ShapeDtypeStruct((b, s, W_D), BF16),
        scratch_shapes=[pltpu.VMEM((VT_ROWS, 2 * ATTN_TQ), F32)],
        compiler_params=_params("parallel", "parallel", "arbitrary"), name="diff_attn",
    )(lam_params, qd, kd, vdt, gain)


def _mix_out_kernel(x_ref, hm_ref, om_ref, od_ref, ga_ref, gb_ref, gm_ref, wa_ref, wb_ref, wo_ref, o_ref):
    hm = hm_ref[...]
    gm = gm_ref[...]
    heads = []
    for h in range(H_M):
        sl = slice(h * DV_M, (h + 1) * DV_M)
        heads.append(_rms(hm[:, sl], gm[:, sl]))
    hn = jnp.concatenate(heads, axis=1) * om_ref[...].astype(F32)
    y_a = jnp.dot(hn.astype(BF16), wa_ref[...], preferred_element_type=F32)
    y_b = jnp.dot(od_ref[...], wb_ref[...], preferred_element_type=F32)
    merged = ga_ref[...].astype(F32) * y_a + gb_ref[...].astype(F32) * y_b
    o_ref[...] = x_ref[...] + jnp.dot(merged.astype(BF16), wo_ref[...], preferred_element_type=F32)


def _mix_out(x2d, hm, om, od, ga, gb, gm, wa, wb, wo):
    t = x2d.shape[0]
    tm = TOKEN_TILE
    row = pl.BlockSpec((tm, D_MODEL), lambda i: (i, 0))
    sq = _const_spec((D_MODEL, D_MODEL))
    return pl.pallas_call(
        _mix_out_kernel, grid=(t // tm,),
        in_specs=[row, row, row, row, row, row, _const_spec((1, W_M)), sq, sq, sq],
        out_specs=row, out_shape=jax.ShapeDtypeStruct((t, D_MODEL), F32),
        compiler_params=_params("parallel"), name="mix_out",
    )(x2d, hm, om, od, ga, gb, gm, wa, wb, wo)


def _mem_kv_kernel(mem_ref, g_ref, wkv_ref, k_o, v_o):
    mn = _rms(mem_ref[0], g_ref[...]).astype(BF16)
    kv = jnp.dot(mn, wkv_ref[...], preferred_element_type=F32)
    k_o[0] = kv[:, :D_MODEL].astype(BF16)
    v_o[0] = kv[:, D_MODEL:].astype(BF16)


def _mem_kv(mem, g, wkv):
    b, m, _ = mem.shape
    blk = pl.BlockSpec((1, m, D_MODEL), lambda bi: (bi, 0, 0))
    return pl.pallas_call(
        _mem_kv_kernel, grid=(b,),
        in_specs=[blk, _const_spec((1, D_MODEL)), _const_spec((D_MODEL, 2 * D_MODEL))],
        out_specs=[blk, blk], out_shape=[jax.ShapeDtypeStruct((b, m, D_MODEL), BF16)] * 2,
        compiler_params=_params("parallel"), name="mem_kv",
    )(mem, g, wkv)


def _cross_attn_kernel(x_ref, g_ref, k_ref, v_ref, wq_ref, wo_ref, o_ref):
    x = x_ref[0]
    u = _rms(x, g_ref[...]).astype(BF16)
    q = jnp.dot(u, wq_ref[...], preferred_element_type=F32).astype(BF16)
    heads = []
    for h in range(H_X):
        sl = slice(h * DH_X, (h + 1) * DH_X)
        s = lax.dot_general(q[:, sl], k_ref[0, :, sl], _NT, preferred_element_type=F32) * (DH_X ** -0.5)
        e = jnp.exp(s - jnp.max(s, axis=1, keepdims=True))
        p = e / jnp.sum(e, axis=1, keepdims=True)
        heads.append(jnp.dot(p.astype(BF16), v_ref[0, :, sl], preferred_element_type=F32))
    o = jnp.concatenate(heads, axis=1).astype(BF16)
    o_ref[0] = x + jnp.dot(o, wo_ref[...], preferred_element_type=F32)


def _cross_attn(x3d, g, kx, vx, wq, wo):
    b, s, _ = x3d.shape
    m = kx.shape[1]
    tm = TOKEN_TILE
    row = pl.BlockSpec((1, tm, D_MODEL), lambda bi, i: (bi, i, 0))
    memb = pl.BlockSpec((1, m, D_MODEL), lambda bi, i: (bi, 0, 0))
    sq = _const_spec((D_MODEL, D_MODEL))
    return pl.pallas_call(
        _cross_attn_kernel, grid=(b, s // tm),
        in_specs=[row, _const_spec((1, D_MODEL)), memb, memb, sq, sq],
        out_specs=row, out_shape=jax.ShapeDtypeStruct((b, s, D_MODEL), F32),
        compiler_params=_params("parallel", "parallel"), name="cross_attn",
    )(x3d, g, kx, vx, wq, wo)


def _rope_tables(s):
    inv_freq = ROPE_THETA ** (-jnp.arange(0, ROT_DIM, 2, dtype=F32) / ROT_DIM)
    ang = jnp.arange(s, dtype=F32)[:, None] * inv_freq[None, :]
    half = ROT_DIM // 2
    one = jnp.ones((s, DK_D - ROT_DIM), F32)
    zero_h = jnp.zeros((s, half), F32)
    zero_r = jnp.zeros((s, DK_D - ROT_DIM), F32)
    cos = jnp.concatenate([jnp.cos(ang), jnp.cos(ang), one], axis=1)
    sa = jnp.concatenate([-jnp.sin(ang), zero_h, zero_r], axis=1)
    sb = jnp.concatenate([zero_h, jnp.sin(ang), zero_r], axis=1)
    return tuple(jnp.concatenate([t, t], axis=1) for t in (cos, sa, sb))


def _prep_weights(ffn1_norm, ffn1_w_in, ffn1_w_out, mix_norm, w_mix_in, b_igate, b_fgate, mlstm_norm, w_branch_a,
                  lambda_q1, lambda_k1, lambda_q2, lambda_k2, diff_norm, w_branch_b, w_mix_out, xattn_norm,
                  mem_norm, w_xq, w_xkv, w_xo, ffn2_norm, ffn2_w_in, ffn2_w_out, final_norm):
    vec = lambda a: a[0].reshape(1, -1).astype(F32)
    bf = lambda a: a.astype(BF16)
    offs = [0]
    for n in SPLIT_SIZES:
        offs.append(offs[-1] + n)
    cols = [w_mix_in[0][:, offs[i]:offs[i + 1]] for i in range(len(SPLIT_SIZES))]
    q_m, k_m, v_m, o_m, ig, fg, q_d, k_d, v_d, g_a, g_b = cols
    gate_cols, gate_bias = [], []
    for h in range(H_M):
        gate_cols += [ig[:, h], ig[:, H_M + h], fg[:, h], fg[:, H_M + h]]
        gate_bias += [b_igate[0, 0, h], b_igate[0, 1, h], b_fgate[0, 0, h], b_fgate[0, 1, h]]
    w_gt = jnp.stack(gate_cols, axis=0)
    b_g = jnp.stack(gate_bias).reshape(4 * H_M, 1).astype(F32)
    mix_w = (bf(q_m), bf(k_m), bf(v_m), bf(o_m), bf(q_d), bf(k_d), bf(v_d.T), bf(g_a), bf(g_b), bf(w_gt), b_g)
    lam_params = jnp.concatenate([lambda_q1, lambda_k1, lambda_q2, lambda_k2], axis=0).astype(F32)
    return dict(
        ffn1=(vec(ffn1_norm), bf(ffn1_w_in[0]), bf(ffn1_w_out[0])),
        ffn2=(vec(ffn2_norm), bf(ffn2_w_in[0]), bf(ffn2_w_out[0])),
        final=final_norm.reshape(1, -1).astype(F32),
        mix_norm=vec(mix_norm), mix_w=mix_w, lam=lam_params,
        diff_gain=diff_norm[0].reshape(H_D, DV_D, 1).astype(F32),
        mix_out=(vec(mlstm_norm), bf(w_branch_a[0]), bf(w_branch_b[0]), bf(w_mix_out[0])),
        xattn=(vec(xattn_norm), vec(mem_norm), bf(w_xq[0]), bf(w_xkv[0]), bf(w_xo[0])),
    )


def _trunk(x, mem, w):
    b, s, d = x.shape
    t = b * s
    lambda_init = 0.8 - 0.6 * math.exp(-0.3 * 0)
    x1 = _ffn(x.reshape(t, d), *w["ffn1"], w["final"], False)
    qm, km, vm, om, qd, kd, vdt, ga, gb, gates = _mix_proj(x1.reshape(b, s, d), w["mix_norm"], _rope_tables(s),
                                                          w["mix_w"])
    hm = _mlstm(qm, km, vm, gates)
    od = _diff_attn(w["lam"], qd, kd, vdt, w["diff_gain"], lambda_init)
    flat = lambda a: a.reshape(t, -1)
    x2 = _mix_out(x1, flat(hm), flat(om), flat(od), flat(ga), flat(gb), *w["mix_out"])
    xg, mg, wq, wkv, wo = w["xattn"]
    kx, vx = _mem_kv(mem, mg, wkv)
    x3 = _cross_attn(x2.reshape(b, s, d), xg, kx, vx, wq, wo)
    y = _ffn(x3.reshape(t, d), *w["ffn2"], w["final"], True)
    return y.reshape(b, s, d)


def kernel(x_prompt, x_sample, mem_prompt, mem_sample, ffn1_norm, ffn1_w_in, ffn1_w_out, mix_norm, w_mix_in, b_igate, b_fgate, mlstm_norm, w_branch_a, lambda_q1, lambda_k1, lambda_q2, lambda_k2, diff_norm, w_branch_b, w_mix_out, xattn_norm, mem_norm, w_xq, w_xkv, w_xo, ffn2_norm, ffn2_w_in, ffn2_w_out, final_norm):
    w = _prep_weights(ffn1_norm, ffn1_w_in, ffn1_w_out, mix_norm, w_mix_in, b_igate, b_fgate, mlstm_norm,
                      w_branch_a, lambda_q1, lambda_k1, lambda_q2, lambda_k2, diff_norm, w_branch_b, w_mix_out,
                      xattn_norm, mem_norm, w_xq, w_xkv, w_xo, ffn2_norm, ffn2_w_in, ffn2_w_out, final_norm)
    return (_trunk(x_prompt, mem_prompt, w), _trunk(x_sample, mem_sample, w))
```

```python
import functools
import math

import jax
import jax.numpy as jnp
from jax import lax
from jax.experimental import pallas as pl
from jax.experimental.pallas import tpu as pltpu

F32 = jnp.float32
BF16 = jnp.bfloat16

D_MODEL = 1024
EPS = 1e-6
LOG2_E = math.log2(math.e)
H_M, DK_M, DV_M = 4, 128, 256
W_M = H_M * DV_M
H_D, DK_D, DV_D = 8, 64, 128
W_D = H_D * DV_D
ROT_DIM = DK_D // 4
ROPE_THETA = 500000.0
H_X = 4
DH_X = D_MODEL // H_X
D_FF = 2816
SPLIT_SIZES = (H_M * DK_M, H_M * DK_M, W_M, W_M, 2 * H_M, 2 * H_M,
               H_D * 2 * DK_D, H_D * 2 * DK_D, W_D, D_MODEL, D_MODEL)

V7X_LANES = 128
V7X_VMEM_LIMIT_BYTES = 56 * 1024 * 1024

TOKEN_TILE = 256
MLSTM_CHUNK = 256
ATTN_TQ = 256
ATTN_TK = 512
ONES_ROWS = 16
VT_ROWS = DV_D + ONES_ROWS


def _params(*sem):
    return pltpu.CompilerParams(dimension_semantics=sem, vmem_limit_bytes=V7X_VMEM_LIMIT_BYTES)


def _const_spec(shape):
    n = len(shape)
    return pl.BlockSpec(shape, lambda *_: (0,) * n, pipeline_mode=pl.Buffered(1))


def _rms(x, g):
    return x * lax.rsqrt(jnp.mean(x * x, axis=-1, keepdims=True) + EPS) * g


def _sigmoid(x):
    return 1.0 / (1.0 + jnp.exp(-x))


_NT = (((1,), (1,)), ((), ()))
_TN = (((0,), (0,)), ((), ()))


def _ffn_kernel(x_ref, g_ref, win_ref, wout_ref, fin_ref, o_ref, *, final_norm):
    x = x_ref[...]
    u = _rms(x, g_ref[...]).astype(BF16)
    z = jnp.dot(u, win_ref[...], preferred_element_type=F32)
    gate, up = z[:, :D_FF], z[:, D_FF:]
    h = (gate * _sigmoid(gate) * up).astype(BF16)
    y = x + 0.5 * jnp.dot(h, wout_ref[...], preferred_element_type=F32)
    if final_norm:
        y = _rms(y, fin_ref[...])
    o_ref[...] = y


def _ffn(x2d, g, w_in, w_out, fin, final_norm):
    t = x2d.shape[0]
    tm = TOKEN_TILE
    row = pl.BlockSpec((tm, D_MODEL), lambda i: (i, 0))
    return pl.pallas_call(
        functools.partial(_ffn_kernel, final_norm=final_norm),
        grid=(t // tm,),
        in_specs=[row, _const_spec((1, D_MODEL)), _const_spec((D_MODEL, 2 * D_FF)),
                  _const_spec((D_FF, D_MODEL)), _const_spec((1, D_MODEL))],
        out_specs=row,
        out_shape=jax.ShapeDtypeStruct((t, D_MODEL), F32),
        compiler_params=_params("parallel"),
        name="ffn_final" if final_norm else "ffn",
    )(x2d, g, w_in, w_out, fin)


def _mix_proj_kernel(x_ref, g_ref, cos_ref, sa_ref, sb_ref, wqm, wkm, wvm, wom, wqd, wkd, wvdt, wga, wgb,
                     wgt, bg_ref, qm_o, km_o, vm_o, om_o, qd_o, kd_o, vdt_o, ga_o, gb_o, gt_o):
    u = _rms(x_ref[0], g_ref[...]).astype(BF16)
    tm = u.shape[0]

    def mm(w):
        return jnp.dot(u, w[...], preferred_element_type=F32)

    qm_o[0] = mm(wqm).astype(BF16)
    km_o[0] = (mm(wkm) * (DK_M ** -0.5)).astype(BF16)
    vm_o[0] = mm(wvm).astype(BF16)
    om_o[0] = _sigmoid(mm(wom)).astype(BF16)
    ga_o[0] = _sigmoid(mm(wga)).astype(BF16)
    gb_o[0] = _sigmoid(mm(wgb)).astype(BF16)

    cos, sa, sb = cos_ref[...], sa_ref[...], sb_ref[...]

    def rope(z):
        heads = []
        for h in range(H_D):
            zh = z[:, h * V7X_LANES:(h + 1) * V7X_LANES]
            heads.append(zh * cos + pltpu.roll(zh, V7X_LANES - ROT_DIM // 2, 1) * sa
                         + pltpu.roll(zh, ROT_DIM // 2, 1) * sb)
        return jnp.concatenate(heads, axis=1)

    qd_o[0] = (rope(mm(wqd)) * (DK_D ** -0.5 * LOG2_E)).astype(BF16)
    kd_o[0] = rope(mm(wkd)).astype(BF16)

    vt = lax.dot_general(wvdt[...], u, _NT, preferred_element_type=F32)
    for h in range(H_D):
        vdt_o[0, h, 0, 0:DV_D, :] = vt[h * DV_D:(h + 1) * DV_D, :].astype(BF16)
        vdt_o[0, h, 0, DV_D:VT_ROWS, :] = jnp.ones((ONES_ROWS, tm), BF16)

    xg = lax.dot_general(wgt[...], u, _NT, preferred_element_type=F32) + bg_ref[...]
    row = lax.broadcasted_iota(jnp.int32, xg.shape, 0)
    log_sig = jnp.minimum(xg, 0.0) - jnp.log1p(jnp.exp(-jnp.abs(xg)))
    gates = jnp.where((row % 4) < 2, xg, log_sig)
    for h in range(H_M):
        gt_o[0, h, 0] = gates[h * 4:(h + 1) * 4, :]


def _mix_proj(x3d, g, tables, w):
    b, s, _ = x3d.shape
    tm = TOKEN_TILE
    assert tm == MLSTM_CHUNK and ATTN_TK % tm == 0
    per_tk = ATTN_TK // tm
    row = lambda n: pl.BlockSpec((1, tm, n), lambda bi, i: (bi, i, 0))
    tab = pl.BlockSpec((tm, V7X_LANES), lambda bi, i: (i, 0))
    bf = lambda n: jax.ShapeDtypeStruct((b, s, n), BF16)
    in_specs = [row(D_MODEL), _const_spec((1, D_MODEL)), tab, tab, tab,
                _const_spec((D_MODEL, H_M * DK_M)), _const_spec((D_MODEL, H_M * DK_M)),
                _const_spec((D_MODEL, W_M)), _const_spec((D_MODEL, W_M)),
                _const_spec((D_MODEL, 2 * H_D * DK_D)), _const_spec((D_MODEL, 2 * H_D * DK_D)),
                _const_spec((W_D, D_MODEL)), _const_spec((D_MODEL, D_MODEL)), _const_spec((D_MODEL, D_MODEL)),
                _const_spec((4 * H_M, D_MODEL)), _const_spec((4 * H_M, 1))]
    out_specs = [row(H_M * DK_M), row(H_M * DK_M), row(W_M), row(W_M), row(2 * H_D * DK_D), row(2 * H_D * DK_D),
                 pl.BlockSpec((1, H_D, 1, VT_ROWS, tm), lambda bi, i: (bi, 0, i // per_tk, 0, i % per_tk)),
                 row(D_MODEL), row(D_MODEL),
                 pl.BlockSpec((1, H_M, 1, 4, tm), lambda bi, i: (bi, 0, i, 0, 0))]
    out_shape = [bf(H_M * DK_M), bf(H_M * DK_M), bf(W_M), bf(W_M), bf(2 * H_D * DK_D), bf(2 * H_D * DK_D),
                 jax.ShapeDtypeStruct((b, H_D, s // ATTN_TK, VT_ROWS, ATTN_TK), BF16),
                 bf(D_MODEL), bf(D_MODEL),
                 jax.ShapeDtypeStruct((b, H_M, s // tm, 4, tm), F32)]
    return pl.pallas_call(
        _mix_proj_kernel, grid=(b, s // tm), in_specs=in_specs, out_specs=out_specs, out_shape=out_shape,
        compiler_params=_params("parallel", "parallel"), name="mix_proj",
    )(x3d, g, *tables, *w)


def _mlstm_chunk(q, k, v, ig_row, lf_row, state, reverse):
    c_t, n, m = state
    L = q.shape[0]
    ri = lax.broadcasted_iota(jnp.int32, (L, L), 0)
    ci = lax.broadcasted_iota(jnp.int32, (L, L), 1)
    tri = (ci >= ri) if reverse else (ci <= ri)
    eye = ci == ri
    b_col = jnp.sum(jnp.where(tri, lf_row, 0.0), axis=1, keepdims=True)
    b_row = jnp.sum(jnp.where(eye, b_col, 0.0), axis=0, keepdims=True)
    ig_col = jnp.sum(jnp.where(eye, ig_row, 0.0), axis=1, keepdims=True)
    b_last = jnp.sum(lf_row, axis=1, keepdims=True)

    dmat = jnp.where(tri, b_col - b_row + ig_row, -jnp.inf)
    inter = b_col + m
    m_row = jnp.maximum(inter, jnp.max(dmat, axis=1, keepdims=True))
    qk = lax.dot_general(q, k, _NT, preferred_element_type=F32)
    w = jnp.exp(dmat - m_row) * qk
    s_inter = jnp.exp(inter - m_row)
    num = (s_inter * jnp.dot(q, c_t.astype(BF16), preferred_element_type=F32)
           + jnp.dot(w.astype(BF16), v, preferred_element_type=F32))
    qn = jnp.sum(q.astype(F32) * n, axis=1, keepdims=True)
    den = s_inter * qn + jnp.sum(w, axis=1, keepdims=True)
    h = num / jnp.maximum(jnp.abs(den), jnp.exp(-m_row))

    a_row = b_last - b_row + ig_row
    m_new = jnp.maximum(b_last + m, jnp.max(a_row, axis=1, keepdims=True))
    decay = jnp.exp(b_last + m - m_new)
    wa_col = jnp.exp(b_last - b_col + ig_col - m_new)
    kw = k.astype(F32) * wa_col
    c_new = decay * c_t + lax.dot_general(kw.astype(BF16), v, _TN, preferred_element_type=F32)
    n_new = decay * n + jnp.sum(kw, axis=0, keepdims=True)
    return h, (c_new, n_new, m_new)


def _mlstm_kernel(q_ref, k_ref, v_ref, g_ref, o_ref, *, n_chunks):
    L = MLSTM_CHUNK
    init = (jnp.zeros((DK_M, DV_M), F32), jnp.zeros((1, DK_M), F32), jnp.zeros((1, 1), F32))

    def run(c, state, reverse):
        rows = pl.ds(pl.multiple_of(c * L, L), L)
        g = g_ref[0, 0, c]
        ig, lf = (g[1:2], g[3:4]) if reverse else (g[0:1], g[2:3])
        return _mlstm_chunk(q_ref[0, rows, :], k_ref[0, rows, :], v_ref[0, rows, :], ig, lf, state, reverse), rows

    def fwd(c, state):
        (h, state), rows = run(c, state, False)
        o_ref[0, rows, :] = h
        return state

    def bwd(i, state):
        (h, state), rows = run(n_chunks - 1 - i, state, True)
        o_ref[0, rows, :] += h
        return state

    lax.fori_loop(0, n_chunks, fwd, init)
    lax.fori_loop(0, n_chunks, bwd, init)


def _mlstm(qm, km, vm, gates):
    b, s, _ = qm.shape
    n_chunks = s // MLSTM_CHUNK
    return pl.pallas_call(
        functools.partial(_mlstm_kernel, n_chunks=n_chunks),
        grid=(b, H_M),
        in_specs=[pl.BlockSpec((1, s, DK_M), lambda bi, h: (bi, 0, h)),
                  pl.BlockSpec((1, s, DK_M), lambda bi, h: (bi, 0, h)),
                  pl.BlockSpec((1, s, DV_M), lambda bi, h: (bi, 0, h)),
                  pl.BlockSpec((1, 1, n_chunks, 4, MLSTM_CHUNK), lambda bi, h: (bi, h, 0, 0, 0))],
        out_specs=pl.BlockSpec((1, s, DV_M), lambda bi, h: (bi, 0, h)),
        out_shape=jax.ShapeDtypeStruct((b, s, W_M), F32),
        compiler_params=_params("parallel", "parallel"), name="mlstm",
    )(qm, km, vm, gates)


def _diff_attn_kernel(lam_ref, q_ref, k_ref, vt_ref, g_ref, o_ref, acc_ref, s0_ref, s1_ref, p0_ref, p1_ref,
                      *, n_kv, lambda_init):
    tq, tk = ATTN_TQ, ATTN_TK
    s_bufs, p_bufs = (s0_ref, s1_ref), (p0_ref, p1_ref)
    qt = q_ref[0].astype(F32).T.astype(BF16)
    row = lax.broadcasted_iota(jnp.int32, qt.shape, 0)
    zero = jnp.zeros_like(qt)
    q_bd = jnp.concatenate([jnp.where(row < DK_D, qt, zero), jnp.where(row >= DK_D, qt, zero)], axis=1)

    def scores(j, slot):
        ks = k_ref[0, j * tk:(j + 1) * tk, :]
        s = jnp.dot(ks, q_bd, preferred_element_type=F32)
        s_bufs[slot][...] = s
        return jnp.max(s, axis=0, keepdims=True)

    def softmax(slot, m, tile_max):
        m_new = jnp.maximum(m, tile_max)
        p_bufs[slot][...] = jnp.exp2(s_bufs[slot][...] - m_new).astype(BF16)
        return m_new, jnp.exp2(m - m_new)

    def accumulate(j, slot, alpha):
        pv = jnp.dot(vt_ref[0, 0, j], p_bufs[slot][...], preferred_element_type=F32)
        acc_ref[...] = acc_ref[...] * alpha + pv

    def step(j, slot, carry):
        m, tile_max, alpha_prev = carry
        next_max = scores(j + 1, 1 - slot)
        m, alpha = softmax(slot, m, tile_max)
        accumulate(j - 1, 1 - slot, alpha_prev)
        return m, next_max, alpha

    acc_ref[...] = jnp.zeros_like(acc_ref)
    tile_max = scores(0, 0)
    next_max = scores(1, 1)
    m, alpha = softmax(0, jnp.full((1, 2 * tq), -jnp.inf, F32), tile_max)

    carry = (m, next_max, alpha)
    for j in range(1, n_kv - 1):
        carry = step(j, j % 2, carry)
    m, tile_max, alpha_prev = carry
    last = n_kv - 1
    m, alpha = softmax(last % 2, m, tile_max)
    accumulate(last - 1, 1 - last % 2, alpha_prev)
    accumulate(last, last % 2, alpha)

    lp = lam_ref[...]
    lam = (jnp.exp(jnp.sum(lp[0:1] * lp[1:2], axis=1, keepdims=True))
           - jnp.exp(jnp.sum(lp[2:3] * lp[3:4], axis=1, keepdims=True)) + lambda_init)
    acc = acc_ref[...]
    row_sum = acc[DV_D:DV_D + 1, :]
    o = acc[:DV_D, :tq] / row_sum[:, :tq] - lam * (acc[:DV_D, tq:] / row_sum[:, tq:])
    o = o * lax.rsqrt(jnp.mean(o * o, axis=0, keepdims=True) + EPS) * g_ref[0]
    o_ref[0] = (o * (1.0 - lambda_init)).T.astype(BF16)


def _diff_attn(lam_params, qd, kd, vdt, gain, lambda_init):
    b, s, _ = qd.shape
    n_kv = s // ATTN_TK
    assert n_kv >= 2 and n_kv % 2 == 0
    return pl.pallas_call(
        functools.partial(_diff_attn_kernel, n_kv=n_kv, lambda_init=lambda_init),
        grid=(b, H_D, s // ATTN_TQ),
        in_specs=[_const_spec((4, DK_D)),
                  pl.BlockSpec((1, ATTN_TQ, 2 * DK_D), lambda bi, h, i: (bi, i, h)),
                  pl.BlockSpec((1, s, 2 * DK_D), lambda bi, h, i: (bi, 0, h)),
                  pl.BlockSpec((1, 1, n_kv, VT_ROWS, ATTN_TK), lambda bi, h, i: (bi, h, 0, 0, 0)),
                  pl.BlockSpec((1, DV_D, 1), lambda bi, h, i: (h, 0, 0))],
        out_specs=pl.BlockSpec((1, ATTN_TQ, DV_D), lambda bi, h, i: (bi, i, h)),
        out_shape=jax.ShapeDtypeStruct((b, s, W_D), BF16),
        scratch_shapes=[pltpu.VMEM((VT_ROWS, 2 * ATTN_TQ), F32),
                        pltpu.VMEM((ATTN_TK, 2 * ATTN_TQ), F32), pltpu.VMEM((ATTN_TK, 2 * ATTN_TQ), F32),
                        pltpu.VMEM((ATTN_TK, 2 * ATTN_TQ), BF16), pltpu.VMEM((ATTN_TK, 2 * ATTN_TQ), BF16)],
        compiler_params=_params("parallel", "parallel", "arbitrary"), name="diff_attn",
    )(lam_params, qd, kd, vdt, gain)


def _mix_out_kernel(x_ref, hm_ref, om_ref, od_ref, ga_ref, gb_ref, gm_ref, wa_ref, wb_ref, wo_ref, o_ref):
    hm = hm_ref[...]
    gm = gm_ref[...]
    heads = []
    for h in range(H_M):
        sl = slice(h * DV_M, (h + 1) * DV_M)
        heads.append(_rms(hm[:, sl], gm[:, sl]))
    hn = jnp.concatenate(heads, axis=1) * om_ref[...].astype(F32)
    y_a = jnp.dot(hn.astype(BF16), wa_ref[...], preferred_element_type=F32)
    y_b = jnp.dot(od_ref[...], wb_ref[...], preferred_element_type=F32)
    merged = ga_ref[...].astype(F32) * y_a + gb_ref[...].astype(F32) * y_b
    o_ref[...] = x_ref[...] + jnp.dot(merged.astype(BF16), wo_ref[...], preferred_element_type=F32)


def _mix_out(x2d, hm, om, od, ga, gb, gm, wa, wb, wo):
    t = x2d.shape[0]
    tm = TOKEN_TILE
    row = pl.BlockSpec((tm, D_MODEL), lambda i: (i, 0))
    sq = _const_spec((D_MODEL, D_MODEL))
    return pl.pallas_call(
        _mix_out_kernel, grid=(t // tm,),
        in_specs=[row, row, row, row, row, row, _const_spec((1, W_M)), sq, sq, sq],
        out_specs=row, out_shape=jax.ShapeDtypeStruct((t, D_MODEL), F32),
        compiler_params=_params("parallel"), name="mix_out",
    )(x2d, hm, om, od, ga, gb, gm, wa, wb, wo)


def _mem_kv_kernel(mem_ref, g_ref, wkv_ref, k_o, v_o):
    mn = _rms(mem_ref[0], g_ref[...]).astype(BF16)
    kv = jnp.dot(mn, wkv_ref[...], preferred_element_type=F32)
    k_o[0] = kv[:, :D_MODEL].astype(BF16)
    v_o[0] = kv[:, D_MODEL:].astype(BF16)


def _mem_kv(mem, g, wkv):
    b, m, _ = mem.shape
    blk = pl.BlockSpec((1, m, D_MODEL), lambda bi: (bi, 0, 0))
    return pl.pallas_call(
        _mem_kv_kernel, grid=(b,),
        in_specs=[blk, _const_spec((1, D_MODEL)), _const_spec((D_MODEL, 2 * D_MODEL))],
        out_specs=[blk, blk], out_shape=[jax.ShapeDtypeStruct((b, m, D_MODEL), BF16)] * 2,
        compiler_params=_params("parallel"), name="mem_kv",
    )(mem, g, wkv)


def _cross_attn_kernel(x_ref, g_ref, k_ref, v_ref, wq_ref, wo_ref, o_ref):
    x = x_ref[0]
    u = _rms(x, g_ref[...]).astype(BF16)
    q = jnp.dot(u, wq_ref[...], preferred_element_type=F32).astype(BF16)
    heads = []
    for h in range(H_X):
        sl = slice(h * DH_X, (h + 1) * DH_X)
        s = lax.dot_general(q[:, sl], k_ref[0, :, sl], _NT, preferred_element_type=F32) * (DH_X ** -0.5)
        e = jnp.exp(s - jnp.max(s, axis=1, keepdims=True))
        p = e / jnp.sum(e, axis=1, keepdims=True)
        heads.append(jnp.dot(p.astype(BF16), v_ref[0, :, sl], preferred_element_type=F32))
    o = jnp.concatenate(heads, axis=1).astype(BF16)
    o_ref[0] = x + jnp.dot(o, wo_ref[...], preferred_element_type=F32)


def _cross_attn(x3d, g, kx, vx, wq, wo):
    b, s, _ = x3d.shape
    m = kx.shape[1]
    tm = TOKEN_TILE
    row = pl.BlockSpec((1, tm, D_MODEL), lambda bi, i: (bi, i, 0))
    memb = pl.BlockSpec((1, m, D_MODEL), lambda bi, i: (bi, 0, 0))
    sq = _const_spec((D_MODEL, D_MODEL))
    return pl.pallas_call(
        _cross_attn_kernel, grid=(b, s // tm),
        in_specs=[row, _const_spec((1, D_MODEL)), memb, memb, sq, sq],
        out_specs=row, out_shape=jax.ShapeDtypeStruct((b, s, D_MODEL), F32),
        compiler_params=_params("parallel", "parallel"), name="cross_attn",
    )(x3d, g, kx, vx, wq, wo)


def _rope_tables(s):
    inv_freq = ROPE_THETA ** (-jnp.arange(0, ROT_DIM, 2, dtype=F32) / ROT_DIM)
    ang = jnp.arange(s, dtype=F32)[:, None] * inv_freq[None, :]
    half = ROT_DIM // 2
    one = jnp.ones((s, DK_D - ROT_DIM), F32)
    zero_h = jnp.zeros((s, half), F32)
    zero_r = jnp.zeros((s, DK_D - ROT_DIM), F32)
    cos = jnp.concatenate([jnp.cos(ang), jnp.cos(ang), one], axis=1)
    sa = jnp.concatenate([-jnp.sin(ang), zero_h, zero_r], axis=1)
    sb = jnp.concatenate([zero_h, jnp.sin(ang), zero_r], axis=1)
    return tuple(jnp.concatenate([t, t], axis=1) for t in (cos, sa, sb))


def _prep_weights(ffn1_norm, ffn1_w_in, ffn1_w_out, mix_norm, w_mix_in, b_igate, b_fgate, mlstm_norm, w_branch_a,
                  lambda_q1, lambda_k1, lambda_q2, lambda_k2, diff_norm, w_branch_b, w_mix_out, xattn_norm,
                  mem_norm, w_xq, w_xkv, w_xo, ffn2_norm, ffn2_w_in, ffn2_w_out, final_norm):
    vec = lambda a: a[0].reshape(1, -1).astype(F32)
    bf = lambda a: a.astype(BF16)
    offs = [0]
    for n in SPLIT_SIZES:
        offs.append(offs[-1] + n)
    cols = [w_mix_in[0][:, offs[i]:offs[i + 1]] for i in range(len(SPLIT_SIZES))]
    q_m, k_m, v_m, o_m, ig, fg, q_d, k_d, v_d, g_a, g_b = cols
    gate_cols, gate_bias = [], []
    for h in range(H_M):
        gate_cols += [ig[:, h], ig[:, H_M + h], fg[:, h], fg[:, H_M + h]]
        gate_bias += [b_igate[0, 0, h], b_igate[0, 1, h], b_fgate[0, 0, h], b_fgate[0, 1, h]]
    w_gt = jnp.stack(gate_cols, axis=0)
    b_g = jnp.stack(gate_bias).reshape(4 * H_M, 1).astype(F32)
    mix_w = (bf(q_m), bf(k_m), bf(v_m), bf(o_m), bf(q_d), bf(k_d), bf(v_d.T), bf(g_a), bf(g_b), bf(w_gt), b_g)
    lam_params = jnp.concatenate([lambda_q1, lambda_k1, lambda_q2, lambda_k2], axis=0).astype(F32)
    return dict(
        ffn1=(vec(ffn1_norm), bf(ffn1_w_in[0]), bf(ffn1_w_out[0])),
        ffn2=(vec(ffn2_norm), bf(ffn2_w_in[0]), bf(ffn2_w_out[0])),
        final=final_norm.reshape(1, -1).astype(F32),
        mix_norm=vec(mix_norm), mix_w=mix_w, lam=lam_params,
        diff_gain=diff_norm[0].reshape(H_D, DV_D, 1).astype(F32),
        mix_out=(vec(mlstm_norm), bf(w_branch_a[0]), bf(w_branch_b[0]), bf(w_mix_out[0])),
        xattn=(vec(xattn_norm), vec(mem_norm), bf(w_xq[0]), bf(w_xkv[0]), bf(w_xo[0])),
    )


def _trunk(x, mem, w):
    b, s, d = x.shape
    t = b * s
    lambda_init = 0.8 - 0.6 * math.exp(-0.3 * 0)
    x1 = _ffn(x.reshape(t, d), *w["ffn1"], w["final"], False)
    qm, km, vm, om, qd, kd, vdt, ga, gb, gates = _mix_proj(x1.reshape(b, s, d), w["mix_norm"], _rope_tables(s),
                                                          w["mix_w"])
    hm = _mlstm(qm, km, vm, gates)
    od = _diff_attn(w["lam"], qd, kd, vdt, w["diff_gain"], lambda_init)
    flat = lambda a: a.reshape(t, -1)
    x2 = _mix_out(x1, flat(hm), flat(om), flat(od), flat(ga), flat(gb), *w["mix_out"])
    xg, mg, wq, wkv, wo = w["xattn"]
    kx, vx = _mem_kv(mem, mg, wkv)
    x3 = _cross_attn(x2.reshape(b, s, d), xg, kx, vx, wq, wo)
    y = _ffn(x3.reshape(t, d), *w["ffn2"], w["final"], True)
    return y.reshape(b, s, d)


def kernel(x_prompt, x_sample, mem_prompt, mem_sample, ffn1_norm, ffn1_w_in, ffn1_w_out, mix_norm, w_mix_in, b_igate, b_fgate, mlstm_norm, w_branch_a, lambda_q1, lambda_k1, lambda_q2, lambda_k2, diff_norm, w_branch_b, w_mix_out, xattn_norm, mem_norm, w_xq, w_xkv, w_xo, ffn2_norm, ffn2_w_in, ffn2_w_out, final_norm):
    w = _prep_weights(ffn1_norm, ffn1_w_in, ffn1_w_out, mix_norm, w_mix_in, b_igate, b_fgate, mlstm_norm,
                      w_branch_a, lambda_q1, lambda_k1, lambda_q2, lambda_k2, diff_norm, w_branch_b, w_mix_out,
                      xattn_norm, mem_norm, w_xq, w_xkv, w_xo, ffn2_norm, ffn2_w_in, ffn2_w_out, final_norm)
    return (_trunk(x_prompt, mem_prompt, w), _trunk(x_sample, mem_sample, w))
```

```python
import functools
import math

import jax
import jax.numpy as jnp
from jax import lax
from jax.experimental import pallas as pl
from jax.experimental.pallas import tpu as pltpu

F32 = jnp.float32
BF16 = jnp.bfloat16

D_MODEL = 1024
EPS = 1e-6
LOG2_E = math.log2(math.e)
H_M, DK_M, DV_M = 4, 128, 256
W_M = H_M * DV_M
H_D, DK_D, DV_D = 8, 64, 128
W_D = H_D * DV_D
ROT_DIM = DK_D // 4
ROPE_THETA = 500000.0
H_X = 4
DH_X = D_MODEL // H_X
D_FF = 2816
SPLIT_SIZES = (H_M * DK_M, H_M * DK_M, W_M, W_M, 2 * H_M, 2 * H_M,
               H_D * 2 * DK_D, H_D * 2 * DK_D, W_D, D_MODEL, D_MODEL)

V7X_LANES = 128
V7X_VMEM_LIMIT_BYTES = 56 * 1024 * 1024

TOKEN_TILE = 256
MLSTM_CHUNK = 256
ATTN_TQ = 512
ATTN_TK = 512
ONES_ROWS = 16
VT_ROWS = DV_D + ONES_ROWS


def _params(*sem):
    return pltpu.CompilerParams(dimension_semantics=sem, vmem_limit_bytes=V7X_VMEM_LIMIT_BYTES)


def _const_spec(shape):
    n = len(shape)
    return pl.BlockSpec(shape, lambda *_: (0,) * n, pipeline_mode=pl.Buffered(1))


def _rms(x, g):
    return x * lax.rsqrt(jnp.mean(x * x, axis=-1, keepdims=True) + EPS) * g


def _sigmoid(x):
    return 1.0 / (1.0 + jnp.exp(-x))


_NT = (((1,), (1,)), ((), ()))
_TN = (((0,), (0,)), ((), ()))


def _ffn_kernel(x_ref, g_ref, win_ref, wout_ref, fin_ref, o_ref, *, final_norm):
    x = x_ref[...]
    u = _rms(x, g_ref[...]).astype(BF16)
    z = jnp.dot(u, win_ref[...], preferred_element_type=F32)
    gate, up = z[:, :D_FF], z[:, D_FF:]
    h = (gate * _sigmoid(gate) * up).astype(BF16)
    y = x + 0.5 * jnp.dot(h, wout_ref[...], preferred_element_type=F32)
    if final_norm:
        y = _rms(y, fin_ref[...])
    o_ref[...] = y


def _ffn(x2d, g, w_in, w_out, fin, final_norm):
    t = x2d.shape[0]
    tm = TOKEN_TILE
    row = pl.BlockSpec((tm, D_MODEL), lambda i: (i, 0))
    return pl.pallas_call(
        functools.partial(_ffn_kernel, final_norm=final_norm),
        grid=(t // tm,),
        in_specs=[row, _const_spec((1, D_MODEL)), _const_spec((D_MODEL, 2 * D_FF)),
                  _const_spec((D_FF, D_MODEL)), _const_spec((1, D_MODEL))],
        out_specs=row,
        out_shape=jax.ShapeDtypeStruct((t, D_MODEL), F32),
        compiler_params=_params("parallel"),
        name="ffn_final" if final_norm else "ffn",
    )(x2d, g, w_in, w_out, fin)


def _mix_proj_kernel(x_ref, g_ref, cos_ref, sa_ref, sb_ref, wqm, wkm, wvm, wom, wqd, wkd, wvdt, wga, wgb,
                     wgt, bg_ref, qm_o, km_o, vm_o, om_o, qd_o, kd_o, vdt_o, ga_o, gb_o, gt_o):
    u = _rms(x_ref[0], g_ref[...]).astype(BF16)
    tm = u.shape[0]

    def mm(w):
        return jnp.dot(u, w[...], preferred_element_type=F32)

    qm_o[0] = mm(wqm).astype(BF16)
    km_o[0] = (mm(wkm) * (DK_M ** -0.5)).astype(BF16)
    vm_o[0] = mm(wvm).astype(BF16)
    om_o[0] = _sigmoid(mm(wom)).astype(BF16)
    ga_o[0] = _sigmoid(mm(wga)).astype(BF16)
    gb_o[0] = _sigmoid(mm(wgb)).astype(BF16)

    cos, sa, sb = cos_ref[...], sa_ref[...], sb_ref[...]

    def rope(z):
        heads = []
        for h in range(H_D):
            zh = z[:, h * V7X_LANES:(h + 1) * V7X_LANES]
            heads.append(zh * cos + pltpu.roll(zh, V7X_LANES - ROT_DIM // 2, 1) * sa
                         + pltpu.roll(zh, ROT_DIM // 2, 1) * sb)
        return jnp.concatenate(heads, axis=1)

    qd_o[0] = (rope(mm(wqd)) * (DK_D ** -0.5 * LOG2_E)).astype(BF16)
    kd_o[0] = rope(mm(wkd)).astype(BF16)

    vt = lax.dot_general(wvdt[...], u, _NT, preferred_element_type=F32)
    for h in range(H_D):
        vdt_o[0, h, 0, 0:DV_D, :] = vt[h * DV_D:(h + 1) * DV_D, :].astype(BF16)
        vdt_o[0, h, 0, DV_D:VT_ROWS, :] = jnp.ones((ONES_ROWS, tm), BF16)

    xg = lax.dot_general(wgt[...], u, _NT, preferred_element_type=F32) + bg_ref[...]
    kind = lax.broadcasted_iota(jnp.int32, xg.shape, 0) % 4
    lane = lax.broadcasted_iota(jnp.int32, xg.shape, 1)
    log_sig = jnp.minimum(xg, 0.0) - jnp.log1p(jnp.exp(-jnp.abs(xg)))
    gates = jnp.where(kind < 2, xg, log_sig) * LOG2_E
    prefix = suffix = gates
    shift = 1
    while shift < tm:
        prefix = prefix + jnp.where(lane >= shift, pltpu.roll(prefix, shift, 1), 0.0)
        suffix = suffix + jnp.where(lane < tm - shift, pltpu.roll(suffix, tm - shift, 1), 0.0)
        shift *= 2
    gates = jnp.where(kind == 2, prefix, jnp.where(kind == 3, suffix, gates))
    for h in range(H_M):
        gt_o[0, h, 0] = gates[h * 4:(h + 1) * 4, :]


def _mix_proj(x3d, g, tables, w):
    b, s, _ = x3d.shape
    tm = TOKEN_TILE
    assert tm == MLSTM_CHUNK and ATTN_TK % tm == 0
    per_tk = ATTN_TK // tm
    row = lambda n: pl.BlockSpec((1, tm, n), lambda bi, i: (bi, i, 0))
    tab = pl.BlockSpec((tm, V7X_LANES), lambda bi, i: (i, 0))
    bf = lambda n: jax.ShapeDtypeStruct((b, s, n), BF16)
    in_specs = [row(D_MODEL), _const_spec((1, D_MODEL)), tab, tab, tab,
                _const_spec((D_MODEL, H_M * DK_M)), _const_spec((D_MODEL, H_M * DK_M)),
                _const_spec((D_MODEL, W_M)), _const_spec((D_MODEL, W_M)),
                _const_spec((D_MODEL, 2 * H_D * DK_D)), _const_spec((D_MODEL, 2 * H_D * DK_D)),
                _const_spec((W_D, D_MODEL)), _const_spec((D_MODEL, D_MODEL)), _const_spec((D_MODEL, D_MODEL)),
                _const_spec((4 * H_M, D_MODEL)), _const_spec((4 * H_M, 1))]
    out_specs = [row(H_M * DK_M), row(H_M * DK_M), row(W_M), row(W_M), row(2 * H_D * DK_D), row(2 * H_D * DK_D),
                 pl.BlockSpec((1, H_D, 1, VT_ROWS, tm), lambda bi, i: (bi, 0, i // per_tk, 0, i % per_tk)),
                 row(D_MODEL), row(D_MODEL),
                 pl.BlockSpec((1, H_M, 1, 4, tm), lambda bi, i: (bi, 0, i, 0, 0))]
    out_shape = [bf(H_M * DK_M), bf(H_M * DK_M), bf(W_M), bf(W_M), bf(2 * H_D * DK_D), bf(2 * H_D * DK_D),
                 jax.ShapeDtypeStruct((b, H_D, s // ATTN_TK, VT_ROWS, ATTN_TK), BF16),
                 bf(D_MODEL), bf(D_MODEL),
                 jax.ShapeDtypeStruct((b, H_M, s // tm, 4, tm), F32)]
    return pl.pallas_call(
        _mix_proj_kernel, grid=(b, s // tm), in_specs=in_specs, out_specs=out_specs, out_shape=out_shape,
        compiler_params=_params("parallel", "parallel"), name="mix_proj",
    )(x3d, g, *tables, *w)


def _mlstm_chunk(q, k, v, ig, b, state, reverse):
    s_ext, m = state
    L, W = q.shape[0], V7X_LANES
    wide = lambda x, n: jnp.concatenate([x] * n, axis=1)
    col = lambda x: jnp.broadcast_to(x, (L, W))
    ri = lax.broadcasted_iota(jnp.int32, (L, L), 0)
    ci = lax.broadcasted_iota(jnp.int32, (L, L), 1)
    tri = (ci >= ri) if reverse else (ci <= ri)
    eye = ci == ri
    r_row = ig - b
    b_last = jnp.broadcast_to(b[:, 0:1] if reverse else b[:, L - 1:L], (1, W))
    r_col = col(jnp.sum(jnp.where(eye, r_row, 0.0), axis=1, keepdims=True))
    b_col = col(jnp.sum(jnp.where(eye, b, 0.0), axis=1, keepdims=True))

    r_tri = jnp.where(tri, r_row, -jnp.inf)
    t = jnp.maximum(m, col(jnp.max(r_tri, axis=1, keepdims=True)))
    qk = lax.dot_general(q, k, _NT, preferred_element_type=F32)
    w = jnp.exp2(r_tri - wide(t, L // W)) * qk
    s_inter = jnp.exp2(m - t)
    qs = jnp.dot(q, s_ext.astype(BF16), preferred_element_type=F32)
    num = wide(s_inter, DV_M // W) * qs[:, :DV_M] + jnp.dot(w.astype(BF16), v, preferred_element_type=F32)
    den = s_inter * qs[:, DV_M:] + col(jnp.sum(w, axis=1, keepdims=True))
    inv = 1.0 / jnp.maximum(jnp.abs(den), jnp.exp2(-(b_col + t)))
    h = num * wide(inv, DV_M // W)

    m_new = jnp.maximum(b_last + m, jnp.max(b_last[:, 0:1] + r_row, axis=1, keepdims=True))
    decay = jnp.exp2(b_last + m - m_new)
    kw = (k.astype(F32) * jnp.exp2(b_last + r_col - m_new)).astype(BF16)
    v_ext = jnp.concatenate([v, jnp.ones((L, W), BF16)], axis=1)
    s_new = wide(decay, DV_M // W + 1) * s_ext + lax.dot_general(kw, v_ext, _TN, preferred_element_type=F32)
    return h, (s_new, m_new)


def _mlstm_kernel(q_ref, k_ref, v_ref, g_ref, o_ref, *, n_chunks):
    L = MLSTM_CHUNK
    init = (jnp.zeros((DK_M, DV_M + V7X_LANES), F32), jnp.zeros((1, V7X_LANES), F32))

    def run(c, state, reverse):
        rows = pl.ds(pl.multiple_of(c * L, L), L)
        g = g_ref[0, 0, c]
        ig, b = (g[1:2], g[3:4]) if reverse else (g[0:1], g[2:3])
        return _mlstm_chunk(q_ref[0, rows, :], k_ref[0, rows, :], v_ref[0, rows, :], ig, b, state, reverse), rows

    def both(c, states, first_touch):
        (h_f, s_f), rows_f = run(c, states[0], False)
        (h_b, s_b), rows_b = run(n_chunks - 1 - c, states[1], True)
        if first_touch:
            o_ref[0, rows_f, :] = h_f
            o_ref[0, rows_b, :] = h_b
        else:
            o_ref[0, rows_f, :] += h_f
            o_ref[0, rows_b, :] += h_b
        return s_f, s_b

    half = n_chunks // 2
    states = lax.fori_loop(0, half, functools.partial(both, first_touch=True), (init, init))
    lax.fori_loop(half, n_chunks, functools.partial(both, first_touch=False), states)


def _mlstm(qm, km, vm, gates):
    b, s, _ = qm.shape
    n_chunks = s // MLSTM_CHUNK
    assert n_chunks % 2 == 0
    return pl.pallas_call(
        functools.partial(_mlstm_kernel, n_chunks=n_chunks),
        grid=(b, H_M),
        in_specs=[pl.BlockSpec((1, s, DK_M), lambda bi, h: (bi, 0, h)),
                  pl.BlockSpec((1, s, DK_M), lambda bi, h: (bi, 0, h)),
                  pl.BlockSpec((1, s, DV_M), lambda bi, h: (bi, 0, h)),
                  pl.BlockSpec((1, 1, n_chunks, 4, MLSTM_CHUNK), lambda bi, h: (bi, h, 0, 0, 0))],
        out_specs=pl.BlockSpec((1, s, DV_M), lambda bi, h: (bi, 0, h)),
        out_shape=jax.ShapeDtypeStruct((b, s, W_M), F32),
        compiler_params=_params("parallel", "parallel"), name="mlstm",
    )(qm, km, vm, gates)


def _diff_attn_kernel(lam_ref, q_ref, k_ref, vt_ref, g_ref, o_ref, acc_ref, s0_ref, s1_ref, p0_ref, p1_ref,
                      *, n_kv, lambda_init):
    tq, tk = ATTN_TQ, ATTN_TK
    s_bufs, p_bufs = (s0_ref, s1_ref), (p0_ref, p1_ref)
    qt = q_ref[0].astype(F32).T.astype(BF16)
    row = lax.broadcasted_iota(jnp.int32, qt.shape, 0)
    zero = jnp.zeros_like(qt)
    q_bd = jnp.concatenate([jnp.where(row < DK_D, qt, zero), jnp.where(row >= DK_D, qt, zero)], axis=1)

    def scores(j, slot):
        ks = k_ref[0, j * tk:(j + 1) * tk, :]
        s = jnp.dot(ks, q_bd, preferred_element_type=F32)
        s_bufs[slot][...] = s
        return jnp.max(s, axis=0, keepdims=True)

    def softmax(slot, m, tile_max):
        m_new = jnp.maximum(m, tile_max)
        p_bufs[slot][...] = jnp.exp2(s_bufs[slot][...] - m_new).astype(BF16)
        return m_new, jnp.exp2(m - m_new)

    def accumulate(j, slot, alpha):
        pv = jnp.dot(vt_ref[0, 0, j], p_bufs[slot][...], preferred_element_type=F32)
        acc_ref[...] = acc_ref[...] * alpha + pv

    def step(j, slot, carry):
        m, tile_max, alpha_prev = carry
        next_max = scores(j + 1, 1 - slot)
        m, alpha = softmax(slot, m, tile_max)
        accumulate(j - 1, 1 - slot, alpha_prev)
        return m, next_max, alpha

    acc_ref[...] = jnp.zeros_like(acc_ref)
    tile_max = scores(0, 0)
    next_max = scores(1, 1)
    m, alpha = softmax(0, jnp.full((1, 2 * tq), -jnp.inf, F32), tile_max)

    carry = (m, next_max, alpha)
    for j in range(1, n_kv - 1):
        carry = step(j, j % 2, carry)
    m, tile_max, alpha_prev = carry
    last = n_kv - 1
    m, alpha = softmax(last % 2, m, tile_max)
    accumulate(last - 1, 1 - last % 2, alpha_prev)
    accumulate(last, last % 2, alpha)

    lp = lam_ref[...]
    lam = (jnp.exp(jnp.sum(lp[0:1] * lp[1:2], axis=1, keepdims=True))
           - jnp.exp(jnp.sum(lp[2:3] * lp[3:4], axis=1, keepdims=True)) + lambda_init)
    acc = acc_ref[...]
    row_sum = acc[DV_D:DV_D + 1, :]
    o = acc[:DV_D, :tq] / row_sum[:, :tq] - lam * (acc[:DV_D, tq:] / row_sum[:, tq:])
    o = o * lax.rsqrt(jnp.mean(o * o, axis=0, keepdims=True) + EPS) * g_ref[0]
    o_ref[0] = (o * (1.0 - lambda_init)).T.astype(BF16)


def _diff_attn(lam_params, qd, kd, vdt, gain, lambda_init):
    b, s, _ = qd.shape
    n_kv = s // ATTN_TK
    assert n_kv >= 2 and n_kv % 2 == 0
    return pl.pallas_call(
        functools.partial(_diff_attn_kernel, n_kv=n_kv, lambda_init=lambda_init),
        grid=(b, H_D, s // ATTN_TQ),
        in_specs=[_const_spec((4, DK_D)),
                  pl.BlockSpec((1, ATTN_TQ, 2 * DK_D), lambda bi, h, i: (bi, i, h)),
                  pl.BlockSpec((1, s, 2 * DK_D), lambda bi, h, i: (bi, 0, h)),
                  pl.BlockSpec((1, 1, n_kv, VT_ROWS, ATTN_TK), lambda bi, h, i: (bi, h, 0, 0, 0)),
                  pl.BlockSpec((1, DV_D, 1), lambda bi, h, i: (h, 0, 0))],
        out_specs=pl.BlockSpec((1, ATTN_TQ, DV_D), lambda bi, h, i: (bi, i, h)),
        out_shape=jax.ShapeDtypeStruct((b, s, W_D), BF16),
        scratch_shapes=[pltpu.VMEM((VT_ROWS, 2 * ATTN_TQ), F32),
                        pltpu.VMEM((ATTN_TK, 2 * ATTN_TQ), F32), pltpu.VMEM((ATTN_TK, 2 * ATTN_TQ), F32),
                        pltpu.VMEM((ATTN_TK, 2 * ATTN_TQ), BF16), pltpu.VMEM((ATTN_TK, 2 * ATTN_TQ), BF16)],
        compiler_params=_params("parallel", "parallel", "arbitrary"), name="diff_attn",
    )(lam_params, qd, kd, vdt, gain)


def _mix_out_kernel(x_ref, hm_ref, om_ref, od_ref, ga_ref, gb_ref, gm_ref, wa_ref, wb_ref, wo_ref, o_ref):
    hm = hm_ref[...]
    gm = gm_ref[...]
    heads = []
    for h in range(H_M):
        sl = slice(h * DV_M, (h + 1) * DV_M)
        heads.append(_rms(hm[:, sl], gm[:, sl]))
    hn = jnp.concatenate(heads, axis=1) * om_ref[...].astype(F32)
    y_a = jnp.dot(hn.astype(BF16), wa_ref[...], preferred_element_type=F32)
    y_b = jnp.dot(od_ref[...], wb_ref[...], preferred_element_type=F32)
    merged = ga_ref[...].astype(F32) * y_a + gb_ref[...].astype(F32) * y_b
    o_ref[...] = x_ref[...] + jnp.dot(merged.astype(BF16), wo_ref[...], preferred_element_type=F32)


def _mix_out(x2d, hm, om, od, ga, gb, gm, wa, wb, wo):
    t = x2d.shape[0]
    tm = TOKEN_TILE
    row = pl.BlockSpec((tm, D_MODEL), lambda i: (i, 0))
    sq = _const_spec((D_MODEL, D_MODEL))
    return pl.pallas_call(
        _mix_out_kernel, grid=(t // tm,),
        in_specs=[row, row, row, row, row, row, _const_spec((1, W_M)), sq, sq, sq],
        out_specs=row, out_shape=jax.ShapeDtypeStruct((t, D_MODEL), F32),
        compiler_params=_params("parallel"), name="mix_out",
    )(x2d, hm, om, od, ga, gb, gm, wa, wb, wo)


def _mem_kv_kernel(mem_ref, g_ref, wkv_ref, k_o, v_o):
    mn = _rms(mem_ref[0], g_ref[...]).astype(BF16)
    kv = jnp.dot(mn, wkv_ref[...], preferred_element_type=F32)
    k_o[0] = kv[:, :D_MODEL].astype(BF16)
    v_o[0] = kv[:, D_MODEL:].astype(BF16)


def _mem_kv(mem, g, wkv):
    b, m, _ = mem.shape
    blk = pl.BlockSpec((1, m, D_MODEL), lambda bi: (bi, 0, 0))
    return pl.pallas_call(
        _mem_kv_kernel, grid=(b,),
        in_specs=[blk, _const_spec((1, D_MODEL)), _const_spec((D_MODEL, 2 * D_MODEL))],
        out_specs=[blk, blk], out_shape=[jax.ShapeDtypeStruct((b, m, D_MODEL), BF16)] * 2,
        compiler_params=_params("parallel"), name="mem_kv",
    )(mem, g, wkv)


def _cross_attn_kernel(x_ref, g_ref, k_ref, v_ref, wq_ref, wo_ref, o_ref):
    x = x_ref[0]
    u = _rms(x, g_ref[...]).astype(BF16)
    q = jnp.dot(u, wq_ref[...], preferred_element_type=F32).astype(BF16)
    heads = []
    for h in range(H_X):
        sl = slice(h * DH_X, (h + 1) * DH_X)
        s = lax.dot_general(q[:, sl], k_ref[0, :, sl], _NT, preferred_element_type=F32) * (DH_X ** -0.5)
        e = jnp.exp(s - jnp.max(s, axis=1, keepdims=True))
        p = e / jnp.sum(e, axis=1, keepdims=True)
        heads.append(jnp.dot(p.astype(BF16), v_ref[0, :, sl], preferred_element_type=F32))
    o = jnp.concatenate(heads, axis=1).astype(BF16)
    o_ref[0] = x + jnp.dot(o, wo_ref[...], preferred_element_type=F32)


def _cross_attn(x3d, g, kx, vx, wq, wo):
    b, s, _ = x3d.shape
    m = kx.shape[1]
    tm = TOKEN_TILE
    row = pl.BlockSpec((1, tm, D_MODEL), lambda bi, i: (bi, i, 0))
    memb = pl.BlockSpec((1, m, D_MODEL), lambda bi, i: (bi, 0, 0))
    sq = _const_spec((D_MODEL, D_MODEL))
    return pl.pallas_call(
        _cross_attn_kernel, grid=(b, s // tm),
        in_specs=[row, _const_spec((1, D_MODEL)), memb, memb, sq, sq],
        out_specs=row, out_shape=jax.ShapeDtypeStruct((b, s, D_MODEL), F32),
        compiler_params=_params("parallel", "parallel"), name="cross_attn",
    )(x3d, g, kx, vx, wq, wo)


def _rope_tables(s):
    inv_freq = ROPE_THETA ** (-jnp.arange(0, ROT_DIM, 2, dtype=F32) / ROT_DIM)
    ang = jnp.arange(s, dtype=F32)[:, None] * inv_freq[None, :]
    half = ROT_DIM // 2
    one = jnp.ones((s, DK_D - ROT_DIM), F32)
    zero_h = jnp.zeros((s, half), F32)
    zero_r = jnp.zeros((s, DK_D - ROT_DIM), F32)
    cos = jnp.concatenate([jnp.cos(ang), jnp.cos(ang), one], axis=1)
    sa = jnp.concatenate([-jnp.sin(ang), zero_h, zero_r], axis=1)
    sb = jnp.concatenate([zero_h, jnp.sin(ang), zero_r], axis=1)
    return tuple(jnp.concatenate([t, t], axis=1) for t in (cos, sa, sb))


def _prep_weights(ffn1_norm, ffn1_w_in, ffn1_w_out, mix_norm, w_mix_in, b_igate, b_fgate, mlstm_norm, w_branch_a,
                  lambda_q1, lambda_k1, lambda_q2, lambda_k2, diff_norm, w_branch_b, w_mix_out, xattn_norm,
                  mem_norm, w_xq, w_xkv, w_xo, ffn2_norm, ffn2_w_in, ffn2_w_out, final_norm):
    vec = lambda a: a[0].reshape(1, -1).astype(F32)
    bf = lambda a: a.astype(BF16)
    offs = [0]
    for n in SPLIT_SIZES:
        offs.append(offs[-1] + n)
    cols = [w_mix_in[0][:, offs[i]:offs[i + 1]] for i in range(len(SPLIT_SIZES))]
    q_m, k_m, v_m, o_m, ig, fg, q_d, k_d, v_d, g_a, g_b = cols
    gate_cols, gate_bias = [], []
    for h in range(H_M):
        gate_cols += [ig[:, h], ig[:, H_M + h], fg[:, h], fg[:, H_M + h]]
        gate_bias += [b_igate[0, 0, h], b_igate[0, 1, h], b_fgate[0, 0, h], b_fgate[0, 1, h]]
    w_gt = jnp.stack(gate_cols, axis=0)
    b_g = jnp.stack(gate_bias).reshape(4 * H_M, 1).astype(F32)
    mix_w = (bf(q_m), bf(k_m), bf(v_m), bf(o_m), bf(q_d), bf(k_d), bf(v_d.T), bf(g_a), bf(g_b), bf(w_gt), b_g)
    lam_params = jnp.concatenate([lambda_q1, lambda_k1, lambda_q2, lambda_k2], axis=0).astype(F32)
    return dict(
        ffn1=(vec(ffn1_norm), bf(ffn1_w_in[0]), bf(ffn1_w_out[0])),
        ffn2=(vec(ffn2_norm), bf(ffn2_w_in[0]), bf(ffn2_w_out[0])),
        final=final_norm.reshape(1, -1).astype(F32),
        mix_norm=vec(mix_norm), mix_w=mix_w, lam=lam_params,
        diff_gain=diff_norm[0].reshape(H_D, DV_D, 1).astype(F32),
        mix_out=(vec(mlstm_norm), bf(w_branch_a[0]), bf(w_branch_b[0]), bf(w_mix_out[0])),
        xattn=(vec(xattn_norm), vec(mem_norm), bf(w_xq[0]), bf(w_xkv[0]), bf(w_xo[0])),
    )


def _trunk(x, mem, w):
    b, s, d = x.shape
    t = b * s
    lambda_init = 0.8 - 0.6 * math.exp(-0.3 * 0)
    x1 = _ffn(x.reshape(t, d), *w["ffn1"], w["final"], False)
    qm, km, vm, om, qd, kd, vdt, ga, gb, gates = _mix_proj(x1.reshape(b, s, d), w["mix_norm"], _rope_tables(s),
                                                          w["mix_w"])
    hm = _mlstm(qm, km, vm, gates)
    od = _diff_attn(w["lam"], qd, kd, vdt, w["diff_gain"], lambda_init)
    flat = lambda a: a.reshape(t, -1)
    x2 = _mix_out(x1, flat(hm), flat(om), flat(od), flat(ga), flat(gb), *w["mix_out"])
    xg, mg, wq, wkv, wo = w["xattn"]
    kx, vx = _mem_kv(mem, mg, wkv)
    x3 = _cross_attn(x2.reshape(b, s, d), xg, kx, vx, wq, wo)
    y = _ffn(x3.reshape(t, d), *w["ffn2"], w["final"], True)
    return y.reshape(b, s, d)


def kernel(x_prompt, x_sample, mem_prompt, mem_sample, ffn1_norm, ffn1_w_in, ffn1_w_out, mix_norm, w_mix_in, b_igate, b_fgate, mlstm_norm, w_branch_a, lambda_q1, lambda_k1, lambda_q2, lambda_k2, diff_norm, w_branch_b, w_mix_out, xattn_norm, mem_norm, w_xq, w_xkv, w_xo, ffn2_norm, ffn2_w_in, ffn2_w_out, final_norm):
    w = _prep_weights(ffn1_norm, ffn1_w_in, ffn1_w_out, mix_norm, w_mix_in, b_igate, b_fgate, mlstm_norm,
                      w_branch_a, lambda_q1, lambda_k1, lambda_q2, lambda_k2, diff_norm, w_branch_b, w_mix_out,
                      xattn_norm, mem_norm, w_xq, w_xkv, w_xo, ffn2_norm, ffn2_w_in, ffn2_w_out, final_norm)
    return (_trunk(x_prompt, mem_prompt, w), _trunk(x_sample, mem_sample, w))
```

```python
import functools
import math

import jax
import jax.numpy as jnp
from jax import lax
from jax.experimental import pallas as pl
from jax.experimental.pallas import tpu as pltpu

F32 = jnp.float32
BF16 = jnp.bfloat16

D_MODEL = 1024
EPS = 1e-6
LOG2_E = math.log2(math.e)
H_M, DK_M, DV_M = 4, 128, 256
W_M = H_M * DV_M
H_D, DK_D, DV_D = 8, 64, 128
W_D = H_D * DV_D
ROT_DIM = DK_D // 4
ROPE_THETA = 500000.0
H_X = 4
DH_X = D_MODEL // H_X
D_FF = 2816
SPLIT_SIZES = (H_M * DK_M, H_M * DK_M, W_M, W_M, 2 * H_M, 2 * H_M,
               H_D * 2 * DK_D, H_D * 2 * DK_D, W_D, D_MODEL, D_MODEL)

V7X_LANES = 128
V7X_VMEM_LIMIT_BYTES = 56 * 1024 * 1024

TOKEN_TILE = 256
MLSTM_CHUNK = 256
ATTN_TQ = 512
ATTN_TK = 512
ONES_ROWS = 16
VT_ROWS = DV_D + ONES_ROWS


def _params(*sem):
    return pltpu.CompilerParams(dimension_semantics=sem, vmem_limit_bytes=V7X_VMEM_LIMIT_BYTES)


def _const_spec(shape):
    n = len(shape)
    return pl.BlockSpec(shape, lambda *_: (0,) * n, pipeline_mode=pl.Buffered(1))


def _rms(x, g):
    return x * lax.rsqrt(jnp.mean(x * x, axis=-1, keepdims=True) + EPS) * g


def _sigmoid(x):
    return 1.0 / (1.0 + jnp.exp(-x))


_NT = (((1,), (1,)), ((), ()))
_TN = (((0,), (0,)), ((), ()))


def _ffn_kernel(x_ref, g_ref, win_ref, wout_ref, o_ref):
    x = x_ref[...]
    u = _rms(x, g_ref[...]).astype(BF16)
    z = jnp.dot(u, win_ref[...], preferred_element_type=F32)
    gate, up = z[:, :D_FF], z[:, D_FF:]
    h = (gate * _sigmoid(gate) * up).astype(BF16)
    o_ref[...] = x + 0.5 * jnp.dot(h, wout_ref[...], preferred_element_type=F32)


def _ffn(x2d, g, w_in, w_out):
    t = x2d.shape[0]
    tm = TOKEN_TILE
    row = pl.BlockSpec((tm, D_MODEL), lambda i: (i, 0))
    return pl.pallas_call(
        _ffn_kernel, grid=(t // tm,),
        in_specs=[row, _const_spec((1, D_MODEL)), _const_spec((D_MODEL, 2 * D_FF)), _const_spec((D_FF, D_MODEL))],
        out_specs=row, out_shape=jax.ShapeDtypeStruct((t, D_MODEL), F32),
        compiler_params=_params("parallel"), name="ffn",
    )(x2d, g, w_in, w_out)


def _mix_proj_kernel(x_ref, g_ref, cos_ref, sa_ref, sb_ref, wqm, wkm, wvm, wom, wqd, wkd, wvdt, wga, wgb,
                     wgt, bg_ref, qm_o, km_o, vm_o, om_o, qd_o, kd_o, vdt_o, ga_o, gb_o, gt_o):
    u = _rms(x_ref[0], g_ref[...]).astype(BF16)
    tm = u.shape[0]

    xg = lax.dot_general(wgt[...], u, _NT, preferred_element_type=F32) + bg_ref[...]
    kind = lax.broadcasted_iota(jnp.int32, xg.shape, 0) % 4
    lane = lax.broadcasted_iota(jnp.int32, xg.shape, 1)
    log_sig = jnp.minimum(xg, 0.0) - jnp.log1p(jnp.exp(-jnp.abs(xg)))
    gates = jnp.where(kind < 2, xg, log_sig) * LOG2_E
    prefix = suffix = gates
    shift = 1
    while shift < tm:
        prefix = prefix + jnp.where(lane >= shift, pltpu.roll(prefix, shift, 1), 0.0)
        suffix = suffix + jnp.where(lane < tm - shift, pltpu.roll(suffix, tm - shift, 1), 0.0)
        shift *= 2
    gates = jnp.where(kind == 2, prefix, jnp.where(kind == 3, suffix, gates))
    for h in range(H_M):
        gt_o[0, h, 0] = gates[h * 4:(h + 1) * 4, :]

    def mm(w):
        return jnp.dot(u, w[...], preferred_element_type=F32)

    qm_o[0] = mm(wqm).astype(BF16)
    km_o[0] = (mm(wkm) * (DK_M ** -0.5)).astype(BF16)
    vm_o[0] = mm(wvm).astype(BF16)
    om_o[0] = _sigmoid(mm(wom)).astype(BF16)
    ga_o[0] = _sigmoid(mm(wga)).astype(BF16)
    gb_o[0] = _sigmoid(mm(wgb)).astype(BF16)

    cos, sa, sb = cos_ref[...], sa_ref[...], sb_ref[...]

    def rope(z):
        heads = []
        for h in range(H_D):
            zh = z[:, h * V7X_LANES:(h + 1) * V7X_LANES]
            heads.append(zh * cos + pltpu.roll(zh, V7X_LANES - ROT_DIM // 2, 1) * sa
                         + pltpu.roll(zh, ROT_DIM // 2, 1) * sb)
        return jnp.concatenate(heads, axis=1)

    qd_o[0] = (rope(mm(wqd)) * (DK_D ** -0.5 * LOG2_E)).astype(BF16)
    kd_o[0] = rope(mm(wkd)).astype(BF16)

    vt = lax.dot_general(wvdt[...], u, _NT, preferred_element_type=F32)
    for h in range(H_D):
        vdt_o[0, h, 0, 0:DV_D, :] = vt[h * DV_D:(h + 1) * DV_D, :].astype(BF16)
        vdt_o[0, h, 0, DV_D:VT_ROWS, :] = jnp.ones((ONES_ROWS, tm), BF16)


def _mix_proj(x3d, g, tables, w):
    b, s, _ = x3d.shape
    tm = TOKEN_TILE
    assert tm == MLSTM_CHUNK and ATTN_TK % tm == 0
    per_tk = ATTN_TK // tm
    row = lambda n: pl.BlockSpec((1, tm, n), lambda bi, i: (bi, i, 0))
    tab = pl.BlockSpec((tm, V7X_LANES), lambda bi, i: (i, 0))
    bf = lambda n: jax.ShapeDtypeStruct((b, s, n), BF16)
    in_specs = [row(D_MODEL), _const_spec((1, D_MODEL)), tab, tab, tab,
                _const_spec((D_MODEL, H_M * DK_M)), _const_spec((D_MODEL, H_M * DK_M)),
                _const_spec((D_MODEL, W_M)), _const_spec((D_MODEL, W_M)),
                _const_spec((D_MODEL, 2 * H_D * DK_D)), _const_spec((D_MODEL, 2 * H_D * DK_D)),
                _const_spec((W_D, D_MODEL)), _const_spec((D_MODEL, D_MODEL)), _const_spec((D_MODEL, D_MODEL)),
                _const_spec((4 * H_M, D_MODEL)), _const_spec((4 * H_M, 1))]
    out_specs = [row(H_M * DK_M), row(H_M * DK_M), row(W_M), row(W_M), row(2 * H_D * DK_D), row(2 * H_D * DK_D),
                 pl.BlockSpec((1, H_D, 1, VT_ROWS, tm), lambda bi, i: (bi, 0, i // per_tk, 0, i % per_tk)),
                 row(D_MODEL), row(D_MODEL),
                 pl.BlockSpec((1, H_M, 1, 4, tm), lambda bi, i: (bi, 0, i, 0, 0))]
    out_shape = [bf(H_M * DK_M), bf(H_M * DK_M), bf(W_M), bf(W_M), bf(2 * H_D * DK_D), bf(2 * H_D * DK_D),
                 jax.ShapeDtypeStruct((b, H_D, s // ATTN_TK, VT_ROWS, ATTN_TK), BF16),
                 bf(D_MODEL), bf(D_MODEL),
                 jax.ShapeDtypeStruct((b, H_M, s // tm, 4, tm), F32)]
    return pl.pallas_call(
        _mix_proj_kernel, grid=(b, s // tm), in_specs=in_specs, out_specs=out_specs, out_shape=out_shape,
        compiler_params=_params("parallel", "parallel"), name="mix_proj",
    )(x3d, g, *tables, *w)


def _mlstm_chunk(q, k, v, ig, b, state, reverse):
    s_ext, m = state
    L, W = q.shape[0], V7X_LANES
    wide = lambda x, n: jnp.concatenate([x] * n, axis=1)
    col = lambda x: jnp.broadcast_to(x, (L, W))
    ri = lax.broadcasted_iota(jnp.int32, (L, L), 0)
    ci = lax.broadcasted_iota(jnp.int32, (L, L), 1)
    tri = (ci >= ri) if reverse else (ci <= ri)
    eye = ci == ri
    r_row = ig - b
    b_last = jnp.broadcast_to(b[:, 0:1] if reverse else b[:, L - 1:L], (1, W))
    r_col = col(jnp.sum(jnp.where(eye, r_row, 0.0), axis=1, keepdims=True))
    b_col = col(jnp.sum(jnp.where(eye, b, 0.0), axis=1, keepdims=True))

    r_tri = jnp.where(tri, r_row, -jnp.inf)
    t = jnp.maximum(m, col(jnp.max(r_tri, axis=1, keepdims=True)))
    qk = lax.dot_general(q, k, _NT, preferred_element_type=F32)
    w = jnp.exp2(r_tri - wide(t, L // W)) * qk
    s_inter = jnp.exp2(m - t)
    qs = jnp.dot(q, s_ext.astype(BF16), preferred_element_type=F32)
    num = wide(s_inter, DV_M // W) * qs[:, :DV_M] + jnp.dot(w.astype(BF16), v, preferred_element_type=F32)
    den = s_inter * qs[:, DV_M:] + col(jnp.sum(w, axis=1, keepdims=True))
    inv = 1.0 / jnp.maximum(jnp.abs(den), jnp.exp2(-(b_col + t)))
    h = num * wide(inv, DV_M // W)

    m_new = jnp.maximum(b_last + m, jnp.max(b_last[:, 0:1] + r_row, axis=1, keepdims=True))
    decay = jnp.exp2(b_last + m - m_new)
    kw = (k.astype(F32) * jnp.exp2(b_last + r_col - m_new)).astype(BF16)
    v_ext = jnp.concatenate([v, jnp.ones((L, W), BF16)], axis=1)
    s_new = wide(decay, DV_M // W + 1) * s_ext + lax.dot_general(kw, v_ext, _TN, preferred_element_type=F32)
    return h, (s_new, m_new)


def _mlstm_kernel(q_ref, k_ref, v_ref, g_ref, o_ref, *, n_chunks):
    L = MLSTM_CHUNK
    init = (jnp.zeros((DK_M, DV_M + V7X_LANES), F32), jnp.zeros((1, V7X_LANES), F32))

    def run(c, state, reverse):
        rows = pl.ds(pl.multiple_of(c * L, L), L)
        g = g_ref[0, 0, c]
        ig, b = (g[1:2], g[3:4]) if reverse else (g[0:1], g[2:3])
        return _mlstm_chunk(q_ref[0, rows, :], k_ref[0, rows, :], v_ref[0, rows, :], ig, b, state, reverse), rows

    def both(c, states, first_touch):
        (h_f, s_f), rows_f = run(c, states[0], False)
        (h_b, s_b), rows_b = run(n_chunks - 1 - c, states[1], True)
        if first_touch:
            o_ref[0, rows_f, :] = h_f
            o_ref[0, rows_b, :] = h_b
        else:
            o_ref[0, rows_f, :] += h_f
            o_ref[0, rows_b, :] += h_b
        return s_f, s_b

    half = n_chunks // 2
    states = lax.fori_loop(0, half, functools.partial(both, first_touch=True), (init, init))
    lax.fori_loop(half, n_chunks, functools.partial(both, first_touch=False), states)


def _mlstm(qm, km, vm, gates):
    b, s, _ = qm.shape
    n_chunks = s // MLSTM_CHUNK
    assert n_chunks % 2 == 0
    return pl.pallas_call(
        functools.partial(_mlstm_kernel, n_chunks=n_chunks),
        grid=(b, H_M),
        in_specs=[pl.BlockSpec((1, s, DK_M), lambda bi, h: (bi, 0, h)),
                  pl.BlockSpec((1, s, DK_M), lambda bi, h: (bi, 0, h)),
                  pl.BlockSpec((1, s, DV_M), lambda bi, h: (bi, 0, h)),
                  pl.BlockSpec((1, 1, n_chunks, 4, MLSTM_CHUNK), lambda bi, h: (bi, h, 0, 0, 0))],
        out_specs=pl.BlockSpec((1, s, DV_M), lambda bi, h: (bi, 0, h)),
        out_shape=jax.ShapeDtypeStruct((b, s, W_M), F32),
        compiler_params=_params("parallel", "parallel"), name="mlstm",
    )(qm, km, vm, gates)


def _diff_attn_kernel(lam_ref, q_ref, k_ref, vt_ref, g_ref, o_ref, acc_ref, s0_ref, s1_ref, p0_ref, p1_ref,
                      *, n_kv, lambda_init):
    tq, tk = ATTN_TQ, ATTN_TK
    s_bufs, p_bufs = (s0_ref, s1_ref), (p0_ref, p1_ref)
    qt = q_ref[0].astype(F32).T.astype(BF16)
    row = lax.broadcasted_iota(jnp.int32, qt.shape, 0)
    zero = jnp.zeros_like(qt)
    q_bd = jnp.concatenate([jnp.where(row < DK_D, qt, zero), jnp.where(row >= DK_D, qt, zero)], axis=1)

    def scores(j, slot):
        ks = k_ref[0, j * tk:(j + 1) * tk, :]
        s = jnp.dot(ks, q_bd, preferred_element_type=F32)
        s_bufs[slot][...] = s
        return jnp.max(s, axis=0, keepdims=True)

    def softmax(slot, m, tile_max):
        m_new = jnp.maximum(m, tile_max)
        p_bufs[slot][...] = jnp.exp2(s_bufs[slot][...] - m_new).astype(BF16)
        return m_new, jnp.exp2(m - m_new)

    def accumulate(j, slot, alpha):
        pv = jnp.dot(vt_ref[0, 0, j], p_bufs[slot][...], preferred_element_type=F32)
        acc_ref[...] = acc_ref[...] * alpha + pv

    def step(j, slot, carry):
        m, tile_max, alpha_prev = carry
        next_max = scores(j + 1, 1 - slot)
        m, alpha = softmax(slot, m, tile_max)
        accumulate(j - 1, 1 - slot, alpha_prev)
        return m, next_max, alpha

    acc_ref[...] = jnp.zeros_like(acc_ref)
    tile_max = scores(0, 0)
    next_max = scores(1, 1)
    m, alpha = softmax(0, jnp.full((1, 2 * tq), -jnp.inf, F32), tile_max)

    carry = (m, next_max, alpha)
    for j in range(1, n_kv - 1):
        carry = step(j, j % 2, carry)
    m, tile_max, alpha_prev = carry
    last = n_kv - 1
    m, alpha = softmax(last % 2, m, tile_max)
    accumulate(last - 1, 1 - last % 2, alpha_prev)
    accumulate(last, last % 2, alpha)

    lp = lam_ref[...]
    lam = (jnp.exp(jnp.sum(lp[0:1] * lp[1:2], axis=1, keepdims=True))
           - jnp.exp(jnp.sum(lp[2:3] * lp[3:4], axis=1, keepdims=True)) + lambda_init)
    acc = acc_ref[...]
    row_sum = acc[DV_D:DV_D + 1, :]
    o = acc[:DV_D, :tq] / row_sum[:, :tq] - lam * (acc[:DV_D, tq:] / row_sum[:, tq:])
    o = o * lax.rsqrt(jnp.mean(o * o, axis=0, keepdims=True) + EPS) * g_ref[0]
    o_ref[0] = (o * (1.0 - lambda_init)).T.astype(BF16)


def _diff_attn(lam_params, qd, kd, vdt, gain, lambda_init):
    b, s, _ = qd.shape
    n_kv = s // ATTN_TK
    assert n_kv >= 2 and n_kv % 2 == 0
    return pl.pallas_call(
        functools.partial(_diff_attn_kernel, n_kv=n_kv, lambda_init=lambda_init),
        grid=(b, H_D, s // ATTN_TQ),
        in_specs=[_const_spec((4, DK_D)),
                  pl.BlockSpec((1, ATTN_TQ, 2 * DK_D), lambda bi, h, i: (bi, i, h)),
                  pl.BlockSpec((1, s, 2 * DK_D), lambda bi, h, i: (bi, 0, h)),
                  pl.BlockSpec((1, 1, n_kv, VT_ROWS, ATTN_TK), lambda bi, h, i: (bi, h, 0, 0, 0)),
                  pl.BlockSpec((1, DV_D, 1), lambda bi, h, i: (h, 0, 0))],
        out_specs=pl.BlockSpec((1, ATTN_TQ, DV_D), lambda bi, h, i: (bi, i, h)),
        out_shape=jax.ShapeDtypeStruct((b, s, W_D), BF16),
        scratch_shapes=[pltpu.VMEM((VT_ROWS, 2 * ATTN_TQ), F32),
                        pltpu.VMEM((ATTN_TK, 2 * ATTN_TQ), F32), pltpu.VMEM((ATTN_TK, 2 * ATTN_TQ), F32),
                        pltpu.VMEM((ATTN_TK, 2 * ATTN_TQ), BF16), pltpu.VMEM((ATTN_TK, 2 * ATTN_TQ), BF16)],
        compiler_params=_params("parallel", "parallel", "arbitrary"), name="diff_attn",
    )(lam_params, qd, kd, vdt, gain)


def _mem_kv_kernel(mem_ref, g_ref, wkv_ref, k_o, v_o):
    mn = _rms(mem_ref[0], g_ref[...]).astype(BF16)
    kv = jnp.dot(mn, wkv_ref[...], preferred_element_type=F32)
    k_o[0] = kv[:, :D_MODEL].astype(BF16)
    v_o[0] = kv[:, D_MODEL:].astype(BF16)


def _mem_kv(mem, g, wkv):
    b, m, _ = mem.shape
    blk = pl.BlockSpec((1, m, D_MODEL), lambda bi: (bi, 0, 0))
    return pl.pallas_call(
        _mem_kv_kernel, grid=(b,),
        in_specs=[blk, _const_spec((1, D_MODEL)), _const_spec((D_MODEL, 2 * D_MODEL))],
        out_specs=[blk, blk], out_shape=[jax.ShapeDtypeStruct((b, m, D_MODEL), BF16)] * 2,
        compiler_params=_params("parallel"), name="mem_kv",
    )(mem, g, wkv)


def _tail_kernel(x_ref, hm_ref, om_ref, od_ref, ga_ref, gb_ref, k_ref, v_ref, gm_ref, wa_ref, wb_ref, wo_ref,
                 xg_ref, wq_ref, wxo_ref, fg_ref, win_ref, wout_ref, fin_ref, o_ref):
    hm = hm_ref[0]
    gm = gm_ref[...]
    heads = []
    for h in range(H_M):
        sl = slice(h * DV_M, (h + 1) * DV_M)
        heads.append(_rms(hm[:, sl], gm[:, sl]))
    hn = jnp.concatenate(heads, axis=1) * om_ref[0].astype(F32)
    y_a = jnp.dot(hn.astype(BF16), wa_ref[...], preferred_element_type=F32)
    y_b = jnp.dot(od_ref[0], wb_ref[...], preferred_element_type=F32)
    merged = ga_ref[0].astype(F32) * y_a + gb_ref[0].astype(F32) * y_b
    x = x_ref[0] + jnp.dot(merged.astype(BF16), wo_ref[...], preferred_element_type=F32)

    u = _rms(x, xg_ref[...]).astype(BF16)
    q = jnp.dot(u, wq_ref[...], preferred_element_type=F32).astype(BF16)
    heads = []
    for h in range(H_X):
        sl = slice(h * DH_X, (h + 1) * DH_X)
        s = lax.dot_general(q[:, sl], k_ref[0, :, sl], _NT, preferred_element_type=F32) * (DH_X ** -0.5)
        e = jnp.exp(s - jnp.max(s, axis=1, keepdims=True))
        p = e / jnp.sum(e, axis=1, keepdims=True)
        heads.append(jnp.dot(p.astype(BF16), v_ref[0, :, sl], preferred_element_type=F32))
    o = jnp.concatenate(heads, axis=1).astype(BF16)
    x = x + jnp.dot(o, wxo_ref[...], preferred_element_type=F32)

    u = _rms(x, fg_ref[...]).astype(BF16)
    z = jnp.dot(u, win_ref[...], preferred_element_type=F32)
    gate, up = z[:, :D_FF], z[:, D_FF:]
    hdn = (gate * _sigmoid(gate) * up).astype(BF16)
    x = x + 0.5 * jnp.dot(hdn, wout_ref[...], preferred_element_type=F32)
    o_ref[0] = _rms(x, fin_ref[...])


def _tail(x3d, hm, om, od, ga, gb, kx, vx, mix_out_w, xattn_w, ffn_w, fin):
    b, s, _ = x3d.shape
    m = kx.shape[1]
    tm = TOKEN_TILE
    gm, wa, wb, wo = mix_out_w
    xg, wq, wxo = xattn_w
    fg, win, wout = ffn_w
    row = pl.BlockSpec((1, tm, D_MODEL), lambda bi, i: (bi, i, 0))
    memb = pl.BlockSpec((1, m, D_MODEL), lambda bi, i: (bi, 0, 0))
    sq = _const_spec((D_MODEL, D_MODEL))
    vec = _const_spec((1, D_MODEL))
    return pl.pallas_call(
        _tail_kernel, grid=(b, s // tm),
        in_specs=[row, row, row, row, row, row, memb, memb, vec, sq, sq, sq, vec, sq, sq, vec,
                  _const_spec((D_MODEL, 2 * D_FF)), _const_spec((D_FF, D_MODEL)), vec],
        out_specs=row, out_shape=jax.ShapeDtypeStruct((b, s, D_MODEL), F32),
        compiler_params=_params("parallel", "parallel"), name="tail",
    )(x3d, hm, om, od, ga, gb, kx, vx, gm, wa, wb, wo, xg, wq, wxo, fg, win, wout, fin)


def _rope_tables(s):
    inv_freq = ROPE_THETA ** (-jnp.arange(0, ROT_DIM, 2, dtype=F32) / ROT_DIM)
    ang = jnp.arange(s, dtype=F32)[:, None] * inv_freq[None, :]
    half = ROT_DIM // 2
    one = jnp.ones((s, DK_D - ROT_DIM), F32)
    zero_h = jnp.zeros((s, half), F32)
    zero_r = jnp.zeros((s, DK_D - ROT_DIM), F32)
    cos = jnp.concatenate([jnp.cos(ang), jnp.cos(ang), one], axis=1)
    sa = jnp.concatenate([-jnp.sin(ang), zero_h, zero_r], axis=1)
    sb = jnp.concatenate([zero_h, jnp.sin(ang), zero_r], axis=1)
    return tuple(jnp.concatenate([t, t], axis=1) for t in (cos, sa, sb))


def _prep_weights(ffn1_norm, ffn1_w_in, ffn1_w_out, mix_norm, w_mix_in, b_igate, b_fgate, mlstm_norm, w_branch_a,
                  lambda_q1, lambda_k1, lambda_q2, lambda_k2, diff_norm, w_branch_b, w_mix_out, xattn_norm,
                  mem_norm, w_xq, w_xkv, w_xo, ffn2_norm, ffn2_w_in, ffn2_w_out, final_norm):
    vec = lambda a: a[0].reshape(1, -1).astype(F32)
    bf = lambda a: a.astype(BF16)
    offs = [0]
    for n in SPLIT_SIZES:
        offs.append(offs[-1] + n)
    cols = [w_mix_in[0][:, offs[i]:offs[i + 1]] for i in range(len(SPLIT_SIZES))]
    q_m, k_m, v_m, o_m, ig, fg, q_d, k_d, v_d, g_a, g_b = cols
    gate_cols, gate_bias = [], []
    for h in range(H_M):
        gate_cols += [ig[:, h], ig[:, H_M + h], fg[:, h], fg[:, H_M + h]]
        gate_bias += [b_igate[0, 0, h], b_igate[0, 1, h], b_fgate[0, 0, h], b_fgate[0, 1, h]]
    w_gt = jnp.stack(gate_cols, axis=0)
    b_g = jnp.stack(gate_bias).reshape(4 * H_M, 1).astype(F32)
    mix_w = (bf(q_m), bf(k_m), bf(v_m), bf(o_m), bf(q_d), bf(k_d), bf(v_d.T), bf(g_a), bf(g_b), bf(w_gt), b_g)
    lam_params = jnp.concatenate([lambda_q1, lambda_k1, lambda_q2, lambda_k2], axis=0).astype(F32)
    return dict(
        ffn1=(vec(ffn1_norm), bf(ffn1_w_in[0]), bf(ffn1_w_out[0])),
        ffn2=(vec(ffn2_norm), bf(ffn2_w_in[0]), bf(ffn2_w_out[0])),
        final=final_norm.reshape(1, -1).astype(F32),
        mix_norm=vec(mix_norm), mix_w=mix_w, lam=lam_params,
        diff_gain=diff_norm[0].reshape(H_D, DV_D, 1).astype(F32),
        mix_out=(vec(mlstm_norm), bf(w_branch_a[0]), bf(w_branch_b[0]), bf(w_mix_out[0])),
        xattn=(vec(xattn_norm), vec(mem_norm), bf(w_xq[0]), bf(w_xkv[0]), bf(w_xo[0])),
    )


def _trunk(x, mem, w):
    b, s, d = x.shape
    t = b * s
    lambda_init = 0.8 - 0.6 * math.exp(-0.3 * 0)
    x1 = _ffn(x.reshape(t, d), *w["ffn1"])
    qm, km, vm, om, qd, kd, vdt, ga, gb, gates = _mix_proj(x1.reshape(b, s, d), w["mix_norm"], _rope_tables(s),
                                                          w["mix_w"])
    hm = _mlstm(qm, km, vm, gates)
    od = _diff_attn(w["lam"], qd, kd, vdt, w["diff_gain"], lambda_init)
    xg, mg, wq, wkv, wo = w["xattn"]
    kx, vx = _mem_kv(mem, mg, wkv)
    return _tail(x1.reshape(b, s, d), hm, om, od, ga, gb, kx, vx, w["mix_out"], (xg, wq, wo), w["ffn2"], w["final"])


def kernel(x_prompt, x_sample, mem_prompt, mem_sample, ffn1_norm, ffn1_w_in, ffn1_w_out, mix_norm, w_mix_in, b_igate, b_fgate, mlstm_norm, w_branch_a, lambda_q1, lambda_k1, lambda_q2, lambda_k2, diff_norm, w_branch_b, w_mix_out, xattn_norm, mem_norm, w_xq, w_xkv, w_xo, ffn2_norm, ffn2_w_in, ffn2_w_out, final_norm):
    w = _prep_weights(ffn1_norm, ffn1_w_in, ffn1_w_out, mix_norm, w_mix_in, b_igate, b_fgate, mlstm_norm,
                      w_branch_a, lambda_q1, lambda_k1, lambda_q2, lambda_k2, diff_norm, w_branch_b, w_mix_out,
                      xattn_norm, mem_norm, w_xq, w_xkv, w_xo, ffn2_norm, ffn2_w_in, ffn2_w_out, final_norm)
    return (_trunk(x_prompt, mem_prompt, w), _trunk(x_sample, mem_sample, w))
```

```python
import functools
import math

import jax
import jax.numpy as jnp
from jax import lax
from jax.experimental import pallas as pl
from jax.experimental.pallas import tpu as pltpu

F32 = jnp.float32
BF16 = jnp.bfloat16

D_MODEL = 1024
EPS = 1e-6
LOG2_E = math.log2(math.e)
H_M, DK_M, DV_M = 4, 128, 256
W_M = H_M * DV_M
H_D, DK_D, DV_D = 8, 64, 128
W_D = H_D * DV_D
ROT_DIM = DK_D // 4
ROPE_THETA = 500000.0
H_X = 4
DH_X = D_MODEL // H_X
D_FF = 2816
SPLIT_SIZES = (H_M * DK_M, H_M * DK_M, W_M, W_M, 2 * H_M, 2 * H_M,
               H_D * 2 * DK_D, H_D * 2 * DK_D, W_D, D_MODEL, D_MODEL)

V7X_LANES = 128
V7X_VMEM_LIMIT_BYTES = 56 * 1024 * 1024

TOKEN_TILE = 256
MLSTM_CHUNK = 256
ATTN_TQ = 512
ATTN_TK = 512
ATTN_TILES_PER_STEP = 2
ONES_ROWS = 16
VT_ROWS = DV_D + ONES_ROWS


def _params(*sem):
    return pltpu.CompilerParams(dimension_semantics=sem, vmem_limit_bytes=V7X_VMEM_LIMIT_BYTES)


def _const_spec(shape):
    n = len(shape)
    return pl.BlockSpec(shape, lambda *_: (0,) * n, pipeline_mode=pl.Buffered(1))


def _rms(x, g):
    return x * lax.rsqrt(jnp.mean(x * x, axis=-1, keepdims=True) + EPS) * g


def _sigmoid(x):
    return 1.0 / (1.0 + jnp.exp(-x))


_NT = (((1,), (1,)), ((), ()))
_TN = (((0,), (0,)), ((), ()))


def _ffn_kernel(x_ref, g_ref, win_ref, wout_ref, o_ref):
    x = x_ref[...]
    u = _rms(x, g_ref[...]).astype(BF16)
    z = jnp.dot(u, win_ref[...], preferred_element_type=F32)
    gate, up = z[:, :D_FF], z[:, D_FF:]
    h = (gate * _sigmoid(gate) * up).astype(BF16)
    o_ref[...] = x + 0.5 * jnp.dot(h, wout_ref[...], preferred_element_type=F32)


def _ffn(x2d, g, w_in, w_out):
    t = x2d.shape[0]
    tm = TOKEN_TILE
    row = pl.BlockSpec((tm, D_MODEL), lambda i: (i, 0))
    return pl.pallas_call(
        _ffn_kernel, grid=(t // tm,),
        in_specs=[row, _const_spec((1, D_MODEL)), _const_spec((D_MODEL, 2 * D_FF)), _const_spec((D_FF, D_MODEL))],
        out_specs=row, out_shape=jax.ShapeDtypeStruct((t, D_MODEL), F32),
        compiler_params=_params("parallel"), name="ffn",
    )(x2d, g, w_in, w_out)


def _mix_proj_kernel(x_ref, g_ref, cos_ref, sa_ref, sb_ref, wqm, wkm, wvm, wom, wqd, wkd, wvdt, wga, wgb,
                     wgt, bg_ref, qm_o, km_o, vm_o, om_o, qd_o, kd_o, vdt_o, ga_o, gb_o, gt_o):
    u = _rms(x_ref[0], g_ref[...]).astype(BF16)
    tm = u.shape[0]

    xg = lax.dot_general(wgt[...], u, _NT, preferred_element_type=F32) + bg_ref[...]
    kind = lax.broadcasted_iota(jnp.int32, xg.shape, 0) % 4
    lane = lax.broadcasted_iota(jnp.int32, xg.shape, 1)
    log_sig = jnp.minimum(xg, 0.0) - jnp.log1p(jnp.exp(-jnp.abs(xg)))
    gates = jnp.where(kind < 2, xg, log_sig) * LOG2_E
    prefix = suffix = gates
    shift = 1
    while shift < tm:
        prefix = prefix + jnp.where(lane >= shift, pltpu.roll(prefix, shift, 1), 0.0)
        suffix = suffix + jnp.where(lane < tm - shift, pltpu.roll(suffix, tm - shift, 1), 0.0)
        shift *= 2
    gates = jnp.where(kind == 2, prefix, jnp.where(kind == 3, suffix, gates))
    for h in range(H_M):
        gt_o[0, h, 0] = gates[h * 4:(h + 1) * 4, :]

    def mm(w):
        return jnp.dot(u, w[...], preferred_element_type=F32)

    qm_o[0] = mm(wqm).astype(BF16)
    km_o[0] = (mm(wkm) * (DK_M ** -0.5)).astype(BF16)
    vm_o[0] = mm(wvm).astype(BF16)
    om_o[0] = _sigmoid(mm(wom)).astype(BF16)
    ga_o[0] = _sigmoid(mm(wga)).astype(BF16)
    gb_o[0] = _sigmoid(mm(wgb)).astype(BF16)

    cos, sa, sb = cos_ref[...], sa_ref[...], sb_ref[...]

    def rope(z):
        heads = []
        for h in range(H_D):
            zh = z[:, h * V7X_LANES:(h + 1) * V7X_LANES]
            heads.append(zh * cos + pltpu.roll(zh, V7X_LANES - ROT_DIM // 2, 1) * sa
                         + pltpu.roll(zh, ROT_DIM // 2, 1) * sb)
        return jnp.concatenate(heads, axis=1)

    qd_o[0] = (rope(mm(wqd)) * (DK_D ** -0.5 * LOG2_E)).astype(BF16)
    kd_o[0] = rope(mm(wkd)).astype(BF16)

    vt = lax.dot_general(wvdt[...], u, _NT, preferred_element_type=F32)
    for h in range(H_D):
        vdt_o[0, h, 0, 0:DV_D, :] = vt[h * DV_D:(h + 1) * DV_D, :].astype(BF16)
        vdt_o[0, h, 0, DV_D:VT_ROWS, :] = jnp.ones((ONES_ROWS, tm), BF16)


def _mix_proj(x3d, g, tables, w):
    b, s, _ = x3d.shape
    tm = TOKEN_TILE
    assert tm == MLSTM_CHUNK and ATTN_TK % tm == 0
    per_tk = ATTN_TK // tm
    row = lambda n: pl.BlockSpec((1, tm, n), lambda bi, i: (bi, i, 0))
    tab = pl.BlockSpec((tm, V7X_LANES), lambda bi, i: (i, 0))
    bf = lambda n: jax.ShapeDtypeStruct((b, s, n), BF16)
    in_specs = [row(D_MODEL), _const_spec((1, D_MODEL)), tab, tab, tab,
                _const_spec((D_MODEL, H_M * DK_M)), _const_spec((D_MODEL, H_M * DK_M)),
                _const_spec((D_MODEL, W_M)), _const_spec((D_MODEL, W_M)),
                _const_spec((D_MODEL, 2 * H_D * DK_D)), _const_spec((D_MODEL, 2 * H_D * DK_D)),
                _const_spec((W_D, D_MODEL)), _const_spec((D_MODEL, D_MODEL)), _const_spec((D_MODEL, D_MODEL)),
                _const_spec((4 * H_M, D_MODEL)), _const_spec((4 * H_M, 1))]
    out_specs = [row(H_M * DK_M), row(H_M * DK_M), row(W_M), row(W_M), row(2 * H_D * DK_D), row(2 * H_D * DK_D),
                 pl.BlockSpec((1, H_D, 1, VT_ROWS, tm), lambda bi, i: (bi, 0, i // per_tk, 0, i % per_tk)),
                 row(D_MODEL), row(D_MODEL),
                 pl.BlockSpec((1, H_M, 1, 4, tm), lambda bi, i: (bi, 0, i, 0, 0))]
    out_shape = [bf(H_M * DK_M), bf(H_M * DK_M), bf(W_M), bf(W_M), bf(2 * H_D * DK_D), bf(2 * H_D * DK_D),
                 jax.ShapeDtypeStruct((b, H_D, s // ATTN_TK, VT_ROWS, ATTN_TK), BF16),
                 bf(D_MODEL), bf(D_MODEL),
                 jax.ShapeDtypeStruct((b, H_M, s // tm, 4, tm), F32)]
    return pl.pallas_call(
        _mix_proj_kernel, grid=(b, s // tm), in_specs=in_specs, out_specs=out_specs, out_shape=out_shape,
        compiler_params=_params("parallel", "parallel"), name="mix_proj",
    )(x3d, g, *tables, *w)


def _mlstm_chunk(q, k, v, ig, b, state, reverse):
    s_ext, m = state
    L, W = q.shape[0], V7X_LANES
    wide = lambda x, n: jnp.concatenate([x] * n, axis=1)
    col = lambda x: jnp.broadcast_to(x, (L, W))
    ri = lax.broadcasted_iota(jnp.int32, (L, L), 0)
    ci = lax.broadcasted_iota(jnp.int32, (L, L), 1)
    tri = (ci >= ri) if reverse else (ci <= ri)
    eye = ci == ri
    r_row = ig - b
    b_last = jnp.broadcast_to(b[:, 0:1] if reverse else b[:, L - 1:L], (1, W))
    r_col = col(jnp.sum(jnp.where(eye, r_row, 0.0), axis=1, keepdims=True))
    b_col = col(jnp.sum(jnp.where(eye, b, 0.0), axis=1, keepdims=True))

    r_tri = jnp.where(tri, r_row, -jnp.inf)
    t = jnp.maximum(m, col(jnp.max(r_tri, axis=1, keepdims=True)))
    qk = lax.dot_general(q, k, _NT, preferred_element_type=F32)
    w = jnp.exp2(r_tri - wide(t, L // W)) * qk
    s_inter = jnp.exp2(m - t)
    qs = jnp.dot(q, s_ext.astype(BF16), preferred_element_type=F32)
    num = wide(s_inter, DV_M // W) * qs[:, :DV_M] + jnp.dot(w.astype(BF16), v, preferred_element_type=F32)
    den = s_inter * qs[:, DV_M:] + col(jnp.sum(w, axis=1, keepdims=True))
    inv = 1.0 / jnp.maximum(jnp.abs(den), jnp.exp2(-(b_col + t)))
    h = num * wide(inv, DV_M // W)

    m_new = jnp.maximum(b_last + m, jnp.max(b_last[:, 0:1] + r_row, axis=1, keepdims=True))
    decay = jnp.exp2(b_last + m - m_new)
    kw = (k.astype(F32) * jnp.exp2(b_last + r_col - m_new)).astype(BF16)
    v_ext = jnp.concatenate([v, jnp.ones((L, W), BF16)], axis=1)
    s_new = wide(decay, DV_M // W + 1) * s_ext + lax.dot_general(kw, v_ext, _TN, preferred_element_type=F32)
    return h, (s_new, m_new)


def _mlstm_kernel(q_ref, k_ref, v_ref, g_ref, o_ref, *, n_chunks):
    L = MLSTM_CHUNK
    init = (jnp.zeros((DK_M, DV_M + V7X_LANES), F32), jnp.zeros((1, V7X_LANES), F32))

    def run(c, state, reverse):
        rows = pl.ds(pl.multiple_of(c * L, L), L)
        g = g_ref[0, 0, c]
        ig, b = (g[1:2], g[3:4]) if reverse else (g[0:1], g[2:3])
        return _mlstm_chunk(q_ref[0, rows, :], k_ref[0, rows, :], v_ref[0, rows, :], ig, b, state, reverse), rows

    def both(c, states, first_touch):
        (h_f, s_f), rows_f = run(c, states[0], False)
        (h_b, s_b), rows_b = run(n_chunks - 1 - c, states[1], True)
        if first_touch:
            o_ref[0, rows_f, :] = h_f
            o_ref[0, rows_b, :] = h_b
        else:
            o_ref[0, rows_f, :] += h_f
            o_ref[0, rows_b, :] += h_b
        return s_f, s_b

    half = n_chunks // 2
    states = lax.fori_loop(0, half, functools.partial(both, first_touch=True), (init, init))
    lax.fori_loop(half, n_chunks, functools.partial(both, first_touch=False), states)


def _mlstm(qm, km, vm, gates):
    b, s, _ = qm.shape
    n_chunks = s // MLSTM_CHUNK
    assert n_chunks % 2 == 0
    return pl.pallas_call(
        functools.partial(_mlstm_kernel, n_chunks=n_chunks),
        grid=(b, H_M),
        in_specs=[pl.BlockSpec((1, s, DK_M), lambda bi, h: (bi, 0, h)),
                  pl.BlockSpec((1, s, DK_M), lambda bi, h: (bi, 0, h)),
                  pl.BlockSpec((1, s, DV_M), lambda bi, h: (bi, 0, h)),
                  pl.BlockSpec((1, 1, n_chunks, 4, MLSTM_CHUNK), lambda bi, h: (bi, h, 0, 0, 0))],
        out_specs=pl.BlockSpec((1, s, DV_M), lambda bi, h: (bi, 0, h)),
        out_shape=jax.ShapeDtypeStruct((b, s, W_M), F32),
        compiler_params=_params("parallel", "parallel"), name="mlstm",
    )(qm, km, vm, gates)


def _diff_attn_kernel(lam_ref, q_ref, k_ref, vt_ref, g_ref, o_ref, *scratch, n_kv, lambda_init):
    acc_refs, stage_refs = scratch[:ATTN_TILES_PER_STEP], scratch[ATTN_TILES_PER_STEP:]
    for tile, acc_ref in enumerate(acc_refs):
        _diff_attn_tile(lam_ref, q_ref, k_ref, vt_ref, g_ref, o_ref, acc_ref, *stage_refs,
                        tile=tile, n_kv=n_kv, lambda_init=lambda_init)


def _diff_attn_tile(lam_ref, q_ref, k_ref, vt_ref, g_ref, o_ref, acc_ref, s0_ref, s1_ref, p0_ref, p1_ref,
                    *, tile, n_kv, lambda_init):
    tq, tk = ATTN_TQ, ATTN_TK
    s_bufs, p_bufs = (s0_ref, s1_ref), (p0_ref, p1_ref)
    qt = q_ref[0, tile * tq:(tile + 1) * tq, :].astype(F32).T.astype(BF16)
    row = lax.broadcasted_iota(jnp.int32, qt.shape, 0)
    zero = jnp.zeros_like(qt)
    q_bd = jnp.concatenate([jnp.where(row < DK_D, qt, zero), jnp.where(row >= DK_D, qt, zero)], axis=1)

    def scores(j, slot):
        ks = k_ref[0, j * tk:(j + 1) * tk, :]
        s = jnp.dot(ks, q_bd, preferred_element_type=F32)
        s_bufs[slot][...] = s
        return jnp.max(s, axis=0, keepdims=True)

    def softmax(slot, m, tile_max):
        m_new = jnp.maximum(m, tile_max)
        p_bufs[slot][...] = jnp.exp2(s_bufs[slot][...] - m_new).astype(BF16)
        return m_new, jnp.exp2(m - m_new)

    def accumulate(j, slot, alpha):
        pv = jnp.dot(vt_ref[0, 0, j], p_bufs[slot][...], preferred_element_type=F32)
        acc_ref[...] = acc_ref[...] * alpha + pv

    def step(j, slot, carry):
        m, tile_max, alpha_prev = carry
        next_max = scores(j + 1, 1 - slot)
        m, alpha = softmax(slot, m, tile_max)
        accumulate(j - 1, 1 - slot, alpha_prev)
        return m, next_max, alpha

    acc_ref[...] = jnp.zeros_like(acc_ref)
    tile_max = scores(0, 0)
    next_max = scores(1, 1)
    m, alpha = softmax(0, jnp.full((1, 2 * tq), -jnp.inf, F32), tile_max)

    carry = (m, next_max, alpha)
    for j in range(1, n_kv - 1):
        carry = step(j, j % 2, carry)
    m, tile_max, alpha_prev = carry
    last = n_kv - 1
    m, alpha = softmax(last % 2, m, tile_max)
    accumulate(last - 1, 1 - last % 2, alpha_prev)
    accumulate(last, last % 2, alpha)

    lp = lam_ref[...]
    lam = (jnp.exp(jnp.sum(lp[0:1] * lp[1:2], axis=1, keepdims=True))
           - jnp.exp(jnp.sum(lp[2:3] * lp[3:4], axis=1, keepdims=True)) + lambda_init)
    acc = acc_ref[...]
    row_sum = acc[DV_D:DV_D + 1, :]
    o = acc[:DV_D, :tq] / row_sum[:, :tq] - lam * (acc[:DV_D, tq:] / row_sum[:, tq:])
    o = o * lax.rsqrt(jnp.mean(o * o, axis=0, keepdims=True) + EPS) * g_ref[0]
    o_ref[0, tile * tq:(tile + 1) * tq, :] = (o * (1.0 - lambda_init)).T.astype(BF16)


def _diff_attn(lam_params, qd, kd, vdt, gain, lambda_init):
    b, s, _ = qd.shape
    n_kv = s // ATTN_TK
    assert n_kv >= 2 and n_kv % 2 == 0
    tq_step = ATTN_TQ * ATTN_TILES_PER_STEP
    return pl.pallas_call(
        functools.partial(_diff_attn_kernel, n_kv=n_kv, lambda_init=lambda_init),
        grid=(b, H_D, s // tq_step),
        in_specs=[_const_spec((4, DK_D)),
                  pl.BlockSpec((1, tq_step, 2 * DK_D), lambda bi, h, i: (bi, i, h)),
                  pl.BlockSpec((1, s, 2 * DK_D), lambda bi, h, i: (bi, 0, h)),
                  pl.BlockSpec((1, 1, n_kv, VT_ROWS, ATTN_TK), lambda bi, h, i: (bi, h, 0, 0, 0)),
                  pl.BlockSpec((1, DV_D, 1), lambda bi, h, i: (h, 0, 0))],
        out_specs=pl.BlockSpec((1, tq_step, DV_D), lambda bi, h, i: (bi, i, h)),
        out_shape=jax.ShapeDtypeStruct((b, s, W_D), BF16),
        scratch_shapes=[pltpu.VMEM((VT_ROWS, 2 * ATTN_TQ), F32)] * ATTN_TILES_PER_STEP + [
                        pltpu.VMEM((ATTN_TK, 2 * ATTN_TQ), F32), pltpu.VMEM((ATTN_TK, 2 * ATTN_TQ), F32),
                        pltpu.VMEM((ATTN_TK, 2 * ATTN_TQ), BF16), pltpu.VMEM((ATTN_TK, 2 * ATTN_TQ), BF16)],
        compiler_params=_params("parallel", "parallel", "arbitrary"), name="diff_attn",
    )(lam_params, qd, kd, vdt, gain)


def _mem_kv_kernel(mem_ref, g_ref, wkv_ref, k_o, v_o):
    mn = _rms(mem_ref[0], g_ref[...]).astype(BF16)
    kv = jnp.dot(mn, wkv_ref[...], preferred_element_type=F32)
    k_o[0] = kv[:, :D_MODEL].astype(BF16)
    v_o[0] = kv[:, D_MODEL:].astype(BF16)


def _mem_kv(mem, g, wkv):
    b, m, _ = mem.shape
    blk = pl.BlockSpec((1, m, D_MODEL), lambda bi: (bi, 0, 0))
    return pl.pallas_call(
        _mem_kv_kernel, grid=(b,),
        in_specs=[blk, _const_spec((1, D_MODEL)), _const_spec((D_MODEL, 2 * D_MODEL))],
        out_specs=[blk, blk], out_shape=[jax.ShapeDtypeStruct((b, m, D_MODEL), BF16)] * 2,
        compiler_params=_params("parallel"), name="mem_kv",
    )(mem, g, wkv)


def _tail_kernel(x_ref, hm_ref, om_ref, od_ref, ga_ref, gb_ref, k_ref, v_ref, gm_ref, wa_ref, wb_ref, wo_ref,
                 xg_ref, wq_ref, wxo_ref, fg_ref, win_ref, wout_ref, fin_ref, o_ref):
    hm = hm_ref[0]
    gm = gm_ref[...]
    heads = []
    for h in range(H_M):
        sl = slice(h * DV_M, (h + 1) * DV_M)
        heads.append(_rms(hm[:, sl], gm[:, sl]))
    hn = jnp.concatenate(heads, axis=1) * om_ref[0].astype(F32)
    y_a = jnp.dot(hn.astype(BF16), wa_ref[...], preferred_element_type=F32)
    y_b = jnp.dot(od_ref[0], wb_ref[...], preferred_element_type=F32)
    merged = ga_ref[0].astype(F32) * y_a + gb_ref[0].astype(F32) * y_b
    x = x_ref[0] + jnp.dot(merged.astype(BF16), wo_ref[...], preferred_element_type=F32)

    u = _rms(x, xg_ref[...]).astype(BF16)
    q = jnp.dot(u, wq_ref[...], preferred_element_type=F32).astype(BF16)
    heads = []
    for h in range(H_X):
        sl = slice(h * DH_X, (h + 1) * DH_X)
        s = lax.dot_general(q[:, sl], k_ref[0, :, sl], _NT, preferred_element_type=F32) * (DH_X ** -0.5)
        e = jnp.exp(s - jnp.max(s, axis=1, keepdims=True))
        p = e / jnp.sum(e, axis=1, keepdims=True)
        heads.append(jnp.dot(p.astype(BF16), v_ref[0, :, sl], preferred_element_type=F32))
    o = jnp.concatenate(heads, axis=1).astype(BF16)
    x = x + jnp.dot(o, wxo_ref[...], preferred_element_type=F32)

    u = _rms(x, fg_ref[...]).astype(BF16)
    z = jnp.dot(u, win_ref[...], preferred_element_type=F32)
    gate, up = z[:, :D_FF], z[:, D_FF:]
    hdn = (gate * _sigmoid(gate) * up).astype(BF16)
    x = x + 0.5 * jnp.dot(hdn, wout_ref[...], preferred_element_type=F32)
    o_ref[0] = _rms(x, fin_ref[...])


def _tail(x3d, hm, om, od, ga, gb, kx, vx, mix_out_w, xattn_w, ffn_w, fin):
    b, s, _ = x3d.shape
    m = kx.shape[1]
    tm = TOKEN_TILE
    gm, wa, wb, wo = mix_out_w
    xg, wq, wxo = xattn_w
    fg, win, wout = ffn_w
    row = pl.BlockSpec((1, tm, D_MODEL), lambda bi, i: (bi, i, 0))
    memb = pl.BlockSpec((1, m, D_MODEL), lambda bi, i: (bi, 0, 0))
    sq = _const_spec((D_MODEL, D_MODEL))
    vec = _const_spec((1, D_MODEL))
    return pl.pallas_call(
        _tail_kernel, grid=(b, s // tm),
        in_specs=[row, row, row, row, row, row, memb, memb, vec, sq, sq, sq, vec, sq, sq, vec,
                  _const_spec((D_MODEL, 2 * D_FF)), _const_spec((D_FF, D_MODEL)), vec],
        out_specs=row, out_shape=jax.ShapeDtypeStruct((b, s, D_MODEL), F32),
        compiler_params=_params("parallel", "parallel"), name="tail",
    )(x3d, hm, om, od, ga, gb, kx, vx, gm, wa, wb, wo, xg, wq, wxo, fg, win, wout, fin)


def _rope_tables(s):
    inv_freq = ROPE_THETA ** (-jnp.arange(0, ROT_DIM, 2, dtype=F32) / ROT_DIM)
    ang = jnp.arange(s, dtype=F32)[:, None] * inv_freq[None, :]
    half = ROT_DIM // 2
    one = jnp.ones((s, DK_D - ROT_DIM), F32)
    zero_h = jnp.zeros((s, half), F32)
    zero_r = jnp.zeros((s, DK_D - ROT_DIM), F32)
    cos = jnp.concatenate([jnp.cos(ang), jnp.cos(ang), one], axis=1)
    sa = jnp.concatenate([-jnp.sin(ang), zero_h, zero_r], axis=1)
    sb = jnp.concatenate([zero_h, jnp.sin(ang), zero_r], axis=1)
    return tuple(jnp.concatenate([t, t], axis=1) for t in (cos, sa, sb))


def _prep_weights(ffn1_norm, ffn1_w_in, ffn1_w_out, mix_norm, w_mix_in, b_igate, b_fgate, mlstm_norm, w_branch_a,
                  lambda_q1, lambda_k1, lambda_q2, lambda_k2, diff_norm, w_branch_b, w_mix_out, xattn_norm,
                  mem_norm, w_xq, w_xkv, w_xo, ffn2_norm, ffn2_w_in, ffn2_w_out, final_norm):
    vec = lambda a: a[0].reshape(1, -1).astype(F32)
    bf = lambda a: a.astype(BF16)
    offs = [0]
    for n in SPLIT_SIZES:
        offs.append(offs[-1] + n)
    cols = [w_mix_in[0][:, offs[i]:offs[i + 1]] for i in range(len(SPLIT_SIZES))]
    q_m, k_m, v_m, o_m, ig, fg, q_d, k_d, v_d, g_a, g_b = cols
    gate_cols, gate_bias = [], []
    for h in range(H_M):
        gate_cols += [ig[:, h], ig[:, H_M + h], fg[:, h], fg[:, H_M + h]]
        gate_bias += [b_igate[0, 0, h], b_igate[0, 1, h], b_fgate[0, 0, h], b_fgate[0, 1, h]]
    w_gt = jnp.stack(gate_cols, axis=0)
    b_g = jnp.stack(gate_bias).reshape(4 * H_M, 1).astype(F32)
    mix_w = (bf(q_m), bf(k_m), bf(v_m), bf(o_m), bf(q_d), bf(k_d), bf(v_d.T), bf(g_a), bf(g_b), bf(w_gt), b_g)
    lam_params = jnp.concatenate([lambda_q1, lambda_k1, lambda_q2, lambda_k2], axis=0).astype(F32)
    return dict(
        ffn1=(vec(ffn1_norm), bf(ffn1_w_in[0]), bf(ffn1_w_out[0])),
        ffn2=(vec(ffn2_norm), bf(ffn2_w_in[0]), bf(ffn2_w_out[0])),
        final=final_norm.reshape(1, -1).astype(F32),
        mix_norm=vec(mix_norm), mix_w=mix_w, lam=lam_params,
        diff_gain=diff_norm[0].reshape(H_D, DV_D, 1).astype(F32),
        mix_out=(vec(mlstm_norm), bf(w_branch_a[0]), bf(w_branch_b[0]), bf(w_mix_out[0])),
        xattn=(vec(xattn_norm), vec(mem_norm), bf(w_xq[0]), bf(w_xkv[0]), bf(w_xo[0])),
    )


def _trunk(x, mem, w):
    b, s, d = x.shape
    t = b * s
    lambda_init = 0.8 - 0.6 * math.exp(-0.3 * 0)
    x1 = _ffn(x.reshape(t, d), *w["ffn1"])
    qm, km, vm, om, qd, kd, vdt, ga, gb, gates = _mix_proj(x1.reshape(b, s, d), w["mix_norm"], _rope_tables(s),
                                                          w["mix_w"])
    hm = _mlstm(qm, km, vm, gates)
    od = _diff_attn(w["lam"], qd, kd, vdt, w["diff_gain"], lambda_init)
    xg, mg, wq, wkv, wo = w["xattn"]
    kx, vx = _mem_kv(mem, mg, wkv)
    return _tail(x1.reshape(b, s, d), hm, om, od, ga, gb, kx, vx, w["mix_out"], (xg, wq, wo), w["ffn2"], w["final"])


def kernel(x_prompt, x_sample, mem_prompt, mem_sample, ffn1_norm, ffn1_w_in, ffn1_w_out, mix_norm, w_mix_in, b_igate, b_fgate, mlstm_norm, w_branch_a, lambda_q1, lambda_k1, lambda_q2, lambda_k2, diff_norm, w_branch_b, w_mix_out, xattn_norm, mem_norm, w_xq, w_xkv, w_xo, ffn2_norm, ffn2_w_in, ffn2_w_out, final_norm):
    w = _prep_weights(ffn1_norm, ffn1_w_in, ffn1_w_out, mix_norm, w_mix_in, b_igate, b_fgate, mlstm_norm,
                      w_branch_a, lambda_q1, lambda_k1, lambda_q2, lambda_k2, diff_norm, w_branch_b, w_mix_out,
                      xattn_norm, mem_norm, w_xq, w_xkv, w_xo, ffn2_norm, ffn2_w_in, ffn2_w_out, final_norm)
    return (_trunk(x_prompt, mem_prompt, w), _trunk(x_sample, mem_sample, w))
```

```python
import functools
import math

import jax
import jax.numpy as jnp
from jax import lax
from jax.experimental import pallas as pl
from jax.experimental.pallas import tpu as pltpu

F32 = jnp.float32
BF16 = jnp.bfloat16

D_MODEL = 1024
EPS = 1e-6
LOG2_E = math.log2(math.e)
H_M, DK_M, DV_M = 4, 128, 256
W_M = H_M * DV_M
H_D, DK_D, DV_D = 8, 64, 128
W_D = H_D * DV_D
ROT_DIM = DK_D // 4
ROPE_THETA = 500000.0
H_X = 4
DH_X = D_MODEL // H_X
D_FF = 2816
SPLIT_SIZES = (H_M * DK_M, H_M * DK_M, W_M, W_M, 2 * H_M, 2 * H_M,
               H_D * 2 * DK_D, H_D * 2 * DK_D, W_D, D_MODEL, D_MODEL)

V7X_LANES = 128
V7X_VMEM_LIMIT_BYTES = 56 * 1024 * 1024

TOKEN_TILE = 256
MLSTM_CHUNK = 256
ATTN_TQ = 512
ATTN_TK = 512
ATTN_TILES_PER_STEP = 2
ATTN_BOUND_SLACK = 1.01
ATTN_MIN_ROW_SUM = 2.0 ** -40
ONES_ROWS = 16
VT_ROWS = DV_D + ONES_ROWS


def _params(*sem):
    return pltpu.CompilerParams(dimension_semantics=sem, vmem_limit_bytes=V7X_VMEM_LIMIT_BYTES)


def _const_spec(shape):
    n = len(shape)
    return pl.BlockSpec(shape, lambda *_: (0,) * n, pipeline_mode=pl.Buffered(1))


def _rms(x, g):
    return x * lax.rsqrt(jnp.mean(x * x, axis=-1, keepdims=True) + EPS) * g


def _sigmoid(x):
    return 1.0 / (1.0 + jnp.exp(-x))


_NT = (((1,), (1,)), ((), ()))
_TN = (((0,), (0,)), ((), ()))


def _ffn_kernel(x_ref, g_ref, win_ref, wout_ref, o_ref):
    x = x_ref[...]
    u = _rms(x, g_ref[...]).astype(BF16)
    z = jnp.dot(u, win_ref[...], preferred_element_type=F32)
    gate, up = z[:, :D_FF], z[:, D_FF:]
    h = (gate * _sigmoid(gate) * up).astype(BF16)
    o_ref[...] = x + 0.5 * jnp.dot(h, wout_ref[...], preferred_element_type=F32)


def _ffn(x2d, g, w_in, w_out):
    t = x2d.shape[0]
    tm = TOKEN_TILE
    row = pl.BlockSpec((tm, D_MODEL), lambda i: (i, 0))
    return pl.pallas_call(
        _ffn_kernel, grid=(t // tm,),
        in_specs=[row, _const_spec((1, D_MODEL)), _const_spec((D_MODEL, 2 * D_FF)), _const_spec((D_FF, D_MODEL))],
        out_specs=row, out_shape=jax.ShapeDtypeStruct((t, D_MODEL), F32),
        compiler_params=_params("parallel"), name="ffn",
    )(x2d, g, w_in, w_out)


def _mix_proj_kernel(x_ref, g_ref, cos_ref, sa_ref, sb_ref, wqm, wkm, wvm, wom, wqd, wkd, wvdt, wga, wgb,
                     wgt, bg_ref, grp_ref, qm_o, km_o, vm_o, om_o, qd_o, kd_o, vdt_o, ga_o, gb_o, gt_o, kn_o):
    u = _rms(x_ref[0], g_ref[...]).astype(BF16)
    tm = u.shape[0]

    xg = lax.dot_general(wgt[...], u, _NT, preferred_element_type=F32) + bg_ref[...]
    kind = lax.broadcasted_iota(jnp.int32, xg.shape, 0) % 4
    lane = lax.broadcasted_iota(jnp.int32, xg.shape, 1)
    log_sig = jnp.minimum(xg, 0.0) - jnp.log1p(jnp.exp(-jnp.abs(xg)))
    gates = jnp.where(kind < 2, xg, log_sig) * LOG2_E
    prefix = suffix = gates
    shift = 1
    while shift < tm:
        prefix = prefix + jnp.where(lane >= shift, pltpu.roll(prefix, shift, 1), 0.0)
        suffix = suffix + jnp.where(lane < tm - shift, pltpu.roll(suffix, tm - shift, 1), 0.0)
        shift *= 2
    gates = jnp.where(kind == 2, prefix, jnp.where(kind == 3, suffix, gates))
    for h in range(H_M):
        gt_o[0, h, 0] = gates[h * 4:(h + 1) * 4, :]

    def mm(w):
        return jnp.dot(u, w[...], preferred_element_type=F32)

    qm_o[0] = mm(wqm).astype(BF16)
    km_o[0] = (mm(wkm) * (DK_M ** -0.5)).astype(BF16)
    vm_o[0] = mm(wvm).astype(BF16)
    om_o[0] = _sigmoid(mm(wom)).astype(BF16)
    ga_o[0] = _sigmoid(mm(wga)).astype(BF16)
    gb_o[0] = _sigmoid(mm(wgb)).astype(BF16)

    cos, sa, sb = cos_ref[...], sa_ref[...], sb_ref[...]

    def rope(z):
        heads = []
        for h in range(H_D):
            zh = z[:, h * V7X_LANES:(h + 1) * V7X_LANES]
            heads.append(zh * cos + pltpu.roll(zh, V7X_LANES - ROT_DIM // 2, 1) * sa
                         + pltpu.roll(zh, ROT_DIM // 2, 1) * sb)
        return jnp.concatenate(heads, axis=1)

    qd_o[0] = (rope(mm(wqd)) * (DK_D ** -0.5 * LOG2_E)).astype(BF16)
    kf = rope(mm(wkd))
    kd_o[0] = kf.astype(BF16)
    norms = jnp.dot((kf * kf).astype(BF16), grp_ref[...], preferred_element_type=F32)
    kn_o[0, 0] = jnp.broadcast_to(jnp.max(norms, axis=0, keepdims=True), kn_o.shape[2:])

    vt = lax.dot_general(wvdt[...], u, _NT, preferred_element_type=F32)
    for h in range(H_D):
        vdt_o[0, h, 0, 0:DV_D, :] = vt[h * DV_D:(h + 1) * DV_D, :].astype(BF16)
        vdt_o[0, h, 0, DV_D:VT_ROWS, :] = jnp.ones((ONES_ROWS, tm), BF16)


def _mix_proj(x3d, g, tables, w):
    b, s, _ = x3d.shape
    tm = TOKEN_TILE
    assert tm == MLSTM_CHUNK and ATTN_TK % tm == 0
    per_tk = ATTN_TK // tm
    row = lambda n: pl.BlockSpec((1, tm, n), lambda bi, i: (bi, i, 0))
    tab = pl.BlockSpec((tm, V7X_LANES), lambda bi, i: (i, 0))
    bf = lambda n: jax.ShapeDtypeStruct((b, s, n), BF16)
    in_specs = [row(D_MODEL), _const_spec((1, D_MODEL)), tab, tab, tab,
                _const_spec((D_MODEL, H_M * DK_M)), _const_spec((D_MODEL, H_M * DK_M)),
                _const_spec((D_MODEL, W_M)), _const_spec((D_MODEL, W_M)),
                _const_spec((D_MODEL, 2 * H_D * DK_D)), _const_spec((D_MODEL, 2 * H_D * DK_D)),
                _const_spec((W_D, D_MODEL)), _const_spec((D_MODEL, D_MODEL)), _const_spec((D_MODEL, D_MODEL)),
                _const_spec((4 * H_M, D_MODEL)), _const_spec((4 * H_M, 1)),
                _const_spec((2 * H_D * DK_D, V7X_LANES))]
    out_specs = [row(H_M * DK_M), row(H_M * DK_M), row(W_M), row(W_M), row(2 * H_D * DK_D), row(2 * H_D * DK_D),
                 pl.BlockSpec((1, H_D, 1, VT_ROWS, tm), lambda bi, i: (bi, 0, i // per_tk, 0, i % per_tk)),
                 row(D_MODEL), row(D_MODEL),
                 pl.BlockSpec((1, H_M, 1, 4, tm), lambda bi, i: (bi, 0, i, 0, 0)),
                 pl.BlockSpec((1, 1, 8, V7X_LANES), lambda bi, i: (bi, i, 0, 0))]
    out_shape = [bf(H_M * DK_M), bf(H_M * DK_M), bf(W_M), bf(W_M), bf(2 * H_D * DK_D), bf(2 * H_D * DK_D),
                 jax.ShapeDtypeStruct((b, H_D, s // ATTN_TK, VT_ROWS, ATTN_TK), BF16),
                 bf(D_MODEL), bf(D_MODEL),
                 jax.ShapeDtypeStruct((b, H_M, s // tm, 4, tm), F32),
                 jax.ShapeDtypeStruct((b, s // tm, 8, V7X_LANES), F32)]
    return pl.pallas_call(
        _mix_proj_kernel, grid=(b, s // tm), in_specs=in_specs, out_specs=out_specs, out_shape=out_shape,
        compiler_params=_params("parallel", "parallel"), name="mix_proj",
    )(x3d, g, *tables, *w)


def _mlstm_chunk(q, k, v, ig, b, state, reverse):
    s_ext, m = state
    L, W = q.shape[0], V7X_LANES
    wide = lambda x, n: jnp.concatenate([x] * n, axis=1)
    col = lambda x: jnp.broadcast_to(x, (L, W))
    ri = lax.broadcasted_iota(jnp.int32, (L, L), 0)
    ci = lax.broadcasted_iota(jnp.int32, (L, L), 1)
    tri = (ci >= ri) if reverse else (ci <= ri)
    eye = ci == ri
    r_row = ig - b
    b_last = jnp.broadcast_to(b[:, 0:1] if reverse else b[:, L - 1:L], (1, W))
    r_col = col(jnp.sum(jnp.where(eye, r_row, 0.0), axis=1, keepdims=True))
    b_col = col(jnp.sum(jnp.where(eye, b, 0.0), axis=1, keepdims=True))

    r_tri = jnp.where(tri, r_row, -jnp.inf)
    t = jnp.maximum(m, col(jnp.max(r_tri, axis=1, keepdims=True)))
    qk = lax.dot_general(q, k, _NT, preferred_element_type=F32)
    w = jnp.exp2(r_tri - wide(t, L // W)) * qk
    s_inter = jnp.exp2(m - t)
    qs = jnp.dot(q, s_ext.astype(BF16), preferred_element_type=F32)
    num = wide(s_inter, DV_M // W) * qs[:, :DV_M] + jnp.dot(w.astype(BF16), v, preferred_element_type=F32)
    den = s_inter * qs[:, DV_M:] + col(jnp.sum(w, axis=1, keepdims=True))
    inv = 1.0 / jnp.maximum(jnp.abs(den), jnp.exp2(-(b_col + t)))
    h = num * wide(inv, DV_M // W)

    m_new = jnp.maximum(b_last + m, jnp.max(b_last[:, 0:1] + r_row, axis=1, keepdims=True))
    decay = jnp.exp2(b_last + m - m_new)
    kw = (k.astype(F32) * jnp.exp2(b_last + r_col - m_new)).astype(BF16)
    v_ext = jnp.concatenate([v, jnp.ones((L, W), BF16)], axis=1)
    s_new = wide(decay, DV_M // W + 1) * s_ext + lax.dot_general(kw, v_ext, _TN, preferred_element_type=F32)
    return h, (s_new, m_new)


def _mlstm_kernel(q_ref, k_ref, v_ref, g_ref, o_ref, *, n_chunks):
    L = MLSTM_CHUNK
    init = (jnp.zeros((DK_M, DV_M + V7X_LANES), F32), jnp.zeros((1, V7X_LANES), F32))

    def run(c, state, reverse):
        rows = pl.ds(pl.multiple_of(c * L, L), L)
        g = g_ref[0, 0, c]
        ig, b = (g[1:2], g[3:4]) if reverse else (g[0:1], g[2:3])
        return _mlstm_chunk(q_ref[0, rows, :], k_ref[0, rows, :], v_ref[0, rows, :], ig, b, state, reverse), rows

    def both(c, states, first_touch):
        (h_f, s_f), rows_f = run(c, states[0], False)
        (h_b, s_b), rows_b = run(n_chunks - 1 - c, states[1], True)
        if first_touch:
            o_ref[0, rows_f, :] = h_f
            o_ref[0, rows_b, :] = h_b
        else:
            o_ref[0, rows_f, :] += h_f
            o_ref[0, rows_b, :] += h_b
        return s_f, s_b

    half = n_chunks // 2
    states = lax.fori_loop(0, half, functools.partial(both, first_touch=True), (init, init))
    lax.fori_loop(half, n_chunks, functools.partial(both, first_touch=False), states)


def _mlstm(qm, km, vm, gates):
    b, s, _ = qm.shape
    n_chunks = s // MLSTM_CHUNK
    assert n_chunks % 2 == 0
    return pl.pallas_call(
        functools.partial(_mlstm_kernel, n_chunks=n_chunks),
        grid=(b, H_M),
        in_specs=[pl.BlockSpec((1, s, DK_M), lambda bi, h: (bi, 0, h)),
                  pl.BlockSpec((1, s, DK_M), lambda bi, h: (bi, 0, h)),
                  pl.BlockSpec((1, s, DV_M), lambda bi, h: (bi, 0, h)),
                  pl.BlockSpec((1, 1, n_chunks, 4, MLSTM_CHUNK), lambda bi, h: (bi, h, 0, 0, 0))],
        out_specs=pl.BlockSpec((1, s, DV_M), lambda bi, h: (bi, 0, h)),
        out_shape=jax.ShapeDtypeStruct((b, s, W_M), F32),
        compiler_params=_params("parallel", "parallel"), name="mlstm",
    )(qm, km, vm, gates)


def _diff_attn_kernel(lam_ref, q_ref, k_ref, vt_ref, g_ref, o_ref, *scratch, n_kv, lambda_init):
    acc_refs, stage_refs = scratch[:ATTN_TILES_PER_STEP], scratch[ATTN_TILES_PER_STEP:]
    for tile, acc_ref in enumerate(acc_refs):
        _diff_attn_tile(lam_ref, q_ref, k_ref, vt_ref, g_ref, o_ref, acc_ref, *stage_refs,
                        tile=tile, n_kv=n_kv, lambda_init=lambda_init)


def _diff_attn_tile(lam_ref, q_ref, k_ref, vt_ref, g_ref, o_ref, acc_ref, s0_ref, s1_ref, p0_ref, p1_ref,
                    *, tile, n_kv, lambda_init):
    tq, tk = ATTN_TQ, ATTN_TK
    s_bufs, p_bufs = (s0_ref, s1_ref), (p0_ref, p1_ref)
    qt = q_ref[0, tile * tq:(tile + 1) * tq, :].astype(F32).T.astype(BF16)
    row = lax.broadcasted_iota(jnp.int32, qt.shape, 0)
    zero = jnp.zeros_like(qt)
    q_bd = jnp.concatenate([jnp.where(row < DK_D, qt, zero), jnp.where(row >= DK_D, qt, zero)], axis=1)

    def scores(j, slot):
        ks = k_ref[0, j * tk:(j + 1) * tk, :]
        s = jnp.dot(ks, q_bd, preferred_element_type=F32)
        s_bufs[slot][...] = s
        return jnp.max(s, axis=0, keepdims=True)

    def softmax(slot, m, tile_max):
        m_new = jnp.maximum(m, tile_max)
        p_bufs[slot][...] = jnp.exp2(s_bufs[slot][...] - m_new).astype(BF16)
        return m_new, jnp.exp2(m - m_new)

    def accumulate(j, slot, alpha):
        pv = jnp.dot(vt_ref[0, 0, j], p_bufs[slot][...], preferred_element_type=F32)
        acc_ref[...] = acc_ref[...] * alpha + pv

    def step(j, slot, carry):
        m, tile_max, alpha_prev = carry
        next_max = scores(j + 1, 1 - slot)
        m, alpha = softmax(slot, m, tile_max)
        accumulate(j - 1, 1 - slot, alpha_prev)
        return m, next_max, alpha

    acc_ref[...] = jnp.zeros_like(acc_ref)
    tile_max = scores(0, 0)
    next_max = scores(1, 1)
    m, alpha = softmax(0, jnp.full((1, 2 * tq), -jnp.inf, F32), tile_max)

    carry = (m, next_max, alpha)
    for j in range(1, n_kv - 1):
        carry = step(j, j % 2, carry)
    m, tile_max, alpha_prev = carry
    last = n_kv - 1
    m, alpha = softmax(last % 2, m, tile_max)
    accumulate(last - 1, 1 - last % 2, alpha_prev)
    accumulate(last, last % 2, alpha)

    lp = lam_ref[...]
    lam = (jnp.exp(jnp.sum(lp[0:1] * lp[1:2], axis=1, keepdims=True))
           - jnp.exp(jnp.sum(lp[2:3] * lp[3:4], axis=1, keepdims=True)) + lambda_init)
    acc = acc_ref[...]
    row_sum = acc[DV_D:DV_D + 1, :]
    o = acc[:DV_D, :tq] / row_sum[:, :tq] - lam * (acc[:DV_D, tq:] / row_sum[:, tq:])
    o = o * lax.rsqrt(jnp.mean(o * o, axis=0, keepdims=True) + EPS) * g_ref[0]
    o_ref[0, tile * tq:(tile + 1) * tq, :] = (o * (1.0 - lambda_init)).T.astype(BF16)


def _diff_attn(lam_params, qd, kd, vdt, gain, lambda_init):
    b, s, _ = qd.shape
    n_kv = s // ATTN_TK
    assert n_kv >= 2 and n_kv % 2 == 0
    tq_step = ATTN_TQ * ATTN_TILES_PER_STEP
    return pl.pallas_call(
        functools.partial(_diff_attn_kernel, n_kv=n_kv, lambda_init=lambda_init),
        grid=(b, H_D, s // tq_step),
        in_specs=[_const_spec((4, DK_D)),
                  pl.BlockSpec((1, tq_step, 2 * DK_D), lambda bi, h, i: (bi, i, h)),
                  pl.BlockSpec((1, s, 2 * DK_D), lambda bi, h, i: (bi, 0, h)),
                  pl.BlockSpec((1, 1, n_kv, VT_ROWS, ATTN_TK), lambda bi, h, i: (bi, h, 0, 0, 0)),
                  pl.BlockSpec((1, DV_D, 1), lambda bi, h, i: (h, 0, 0))],
        out_specs=pl.BlockSpec((1, tq_step, DV_D), lambda bi, h, i: (bi, i, h)),
        out_shape=jax.ShapeDtypeStruct((b, s, W_D), BF16),
        scratch_shapes=[pltpu.VMEM((VT_ROWS, 2 * ATTN_TQ), F32)] * ATTN_TILES_PER_STEP + [
                        pltpu.VMEM((ATTN_TK, 2 * ATTN_TQ), F32), pltpu.VMEM((ATTN_TK, 2 * ATTN_TQ), F32),
                        pltpu.VMEM((ATTN_TK, 2 * ATTN_TQ), BF16), pltpu.VMEM((ATTN_TK, 2 * ATTN_TQ), BF16)],
        compiler_params=_params("parallel", "parallel", "arbitrary"), name="diff_attn",
    )(lam_params, qd, kd, vdt, gain)


def _diff_attn_fast_kernel(lam_ref, q_ref, k_ref, vt_ref, g_ref, kb_ref, o_ref, low_ref, *scratch,
                           n_kv, lambda_init):
    acc_refs, p_refs = scratch[:ATTN_TILES_PER_STEP], scratch[ATTN_TILES_PER_STEP:]
    lows = [_diff_attn_fast_tile(lam_ref, q_ref, k_ref, vt_ref, g_ref, kb_ref, o_ref, acc_ref, *p_refs,
                                 tile=tile, n_kv=n_kv, lambda_init=lambda_init)
            for tile, acc_ref in enumerate(acc_refs)]
    low = lows[0]
    for other in lows[1:]:
        low = jnp.minimum(low, other)
    low_ref[0, 0, 0] = jnp.broadcast_to(low, low_ref.shape[3:])


def _diff_attn_fast_tile(lam_ref, q_ref, k_ref, vt_ref, g_ref, kb_ref, o_ref, acc_ref, p0_ref, p1_ref,
                         *, tile, n_kv, lambda_init):
    tq, tk = ATTN_TQ, ATTN_TK
    p_bufs = (p0_ref, p1_ref)
    qf = q_ref[0, tile * tq:(tile + 1) * tq, :].astype(F32).T
    row = lax.broadcasted_iota(jnp.int32, qf.shape, 0)
    sq = qf * qf
    n1 = jnp.sqrt(jnp.sum(jnp.where(row < DK_D, sq, 0.0), axis=0, keepdims=True))
    n2 = jnp.sqrt(jnp.sum(jnp.where(row >= DK_D, sq, 0.0), axis=0, keepdims=True))
    kb = kb_ref[0, 0]
    wide = lambda x: jnp.concatenate([x] * (tq // V7X_LANES), axis=1)
    neg = -jnp.concatenate([n1 * wide(kb[0:1]), n2 * wide(kb[1:2])], axis=1)
    hi = neg.astype(BF16).astype(F32)
    mid = (neg - hi).astype(BF16).astype(F32)
    lo = (neg - hi - mid).astype(BF16).astype(F32)
    srow = lax.broadcasted_iota(jnp.int32, (V7X_LANES, 2 * tq), 0)
    shift = jnp.where(srow == 0, hi, jnp.where(srow == 1, mid, jnp.where(srow == 2, lo, 0.0))).astype(BF16)
    qt = qf.astype(BF16)
    zero = jnp.zeros_like(qt)
    q_bd = jnp.concatenate([jnp.where(row < DK_D, qt, zero), jnp.where(row >= DK_D, qt, zero)], axis=1)
    q_aug = jnp.concatenate([q_bd, shift], axis=0)
    ones = jnp.ones((tk, V7X_LANES), BF16)

    def scores(j, slot):
        ks = jnp.concatenate([k_ref[0, j * tk:(j + 1) * tk, :], ones], axis=1)
        s = jnp.dot(ks, q_aug, preferred_element_type=F32)
        p_bufs[slot][...] = jnp.exp2(s).astype(BF16)

    def accumulate(j, slot, first):
        pv = jnp.dot(vt_ref[0, 0, j], p_bufs[slot][...], preferred_element_type=F32)
        acc_ref[...] = pv if first else acc_ref[...] + pv

    scores(0, 0)
    for j in range(1, n_kv):
        scores(j, j % 2)
        accumulate(j - 1, (j - 1) % 2, j == 1)
    accumulate(n_kv - 1, (n_kv - 1) % 2, False)

    lp = lam_ref[...]
    lam = (jnp.exp(jnp.sum(lp[0:1] * lp[1:2], axis=1, keepdims=True))
           - jnp.exp(jnp.sum(lp[2:3] * lp[3:4], axis=1, keepdims=True)) + lambda_init)
    acc = acc_ref[...]
    row_sum = acc[DV_D:DV_D + 1, :]
    o = acc[:DV_D, :tq] / row_sum[:, :tq] - lam * (acc[:DV_D, tq:] / row_sum[:, tq:])
    o = o * lax.rsqrt(jnp.mean(o * o, axis=0, keepdims=True) + EPS) * g_ref[0]
    o_ref[0, tile * tq:(tile + 1) * tq, :] = (o * (1.0 - lambda_init)).T.astype(BF16)
    return jnp.min(row_sum, axis=1, keepdims=True)


def _diff_attn_fast(lam_params, qd, kd, vdt, gain, key_bound, lambda_init):
    b, s, _ = qd.shape
    n_kv = s // ATTN_TK
    assert n_kv >= 2
    tq_step = ATTN_TQ * ATTN_TILES_PER_STEP
    n_steps = s // tq_step
    return pl.pallas_call(
        functools.partial(_diff_attn_fast_kernel, n_kv=n_kv, lambda_init=lambda_init),
        grid=(b, H_D, n_steps),
        in_specs=[_const_spec((4, DK_D)),
                  pl.BlockSpec((1, tq_step, 2 * DK_D), lambda bi, h, i: (bi, i, h)),
                  pl.BlockSpec((1, s, 2 * DK_D), lambda bi, h, i: (bi, 0, h)),
                  pl.BlockSpec((1, 1, n_kv, VT_ROWS, ATTN_TK), lambda bi, h, i: (bi, h, 0, 0, 0)),
                  pl.BlockSpec((1, DV_D, 1), lambda bi, h, i: (h, 0, 0)),
                  pl.BlockSpec((1, 1, 8, V7X_LANES), lambda bi, h, i: (bi, h, 0, 0))],
        out_specs=[pl.BlockSpec((1, tq_step, DV_D), lambda bi, h, i: (bi, i, h)),
                   pl.BlockSpec((1, 1, 1, 8, V7X_LANES), lambda bi, h, i: (bi, h, i, 0, 0))],
        out_shape=[jax.ShapeDtypeStruct((b, s, W_D), BF16),
                   jax.ShapeDtypeStruct((b, H_D, n_steps, 8, V7X_LANES), F32)],
        scratch_shapes=[pltpu.VMEM((VT_ROWS, 2 * ATTN_TQ), F32)] * ATTN_TILES_PER_STEP + [
                        pltpu.VMEM((ATTN_TK, 2 * ATTN_TQ), BF16), pltpu.VMEM((ATTN_TK, 2 * ATTN_TQ), BF16)],
        compiler_params=_params("parallel", "parallel", "arbitrary"), name="diff_attn_fast",
    )(lam_params, qd, kd, vdt, gain, key_bound)


def _mem_kv_kernel(mem_ref, g_ref, wkv_ref, k_o, v_o):
    mn = _rms(mem_ref[0], g_ref[...]).astype(BF16)
    kv = jnp.dot(mn, wkv_ref[...], preferred_element_type=F32)
    k_o[0] = kv[:, :D_MODEL].astype(BF16)
    v_o[0] = kv[:, D_MODEL:].astype(BF16)


def _mem_kv(mem, g, wkv):
    b, m, _ = mem.shape
    blk = pl.BlockSpec((1, m, D_MODEL), lambda bi: (bi, 0, 0))
    return pl.pallas_call(
        _mem_kv_kernel, grid=(b,),
        in_specs=[blk, _const_spec((1, D_MODEL)), _const_spec((D_MODEL, 2 * D_MODEL))],
        out_specs=[blk, blk], out_shape=[jax.ShapeDtypeStruct((b, m, D_MODEL), BF16)] * 2,
        compiler_params=_params("parallel"), name="mem_kv",
    )(mem, g, wkv)


def _tail_kernel(x_ref, hm_ref, om_ref, od_ref, ga_ref, gb_ref, k_ref, v_ref, gm_ref, wa_ref, wb_ref, wo_ref,
                 xg_ref, wq_ref, wxo_ref, fg_ref, win_ref, wout_ref, fin_ref, o_ref):
    hm = hm_ref[0]
    gm = gm_ref[...]
    heads = []
    for h in range(H_M):
        sl = slice(h * DV_M, (h + 1) * DV_M)
        heads.append(_rms(hm[:, sl], gm[:, sl]))
    hn = jnp.concatenate(heads, axis=1) * om_ref[0].astype(F32)
    y_a = jnp.dot(hn.astype(BF16), wa_ref[...], preferred_element_type=F32)
    y_b = jnp.dot(od_ref[0], wb_ref[...], preferred_element_type=F32)
    merged = ga_ref[0].astype(F32) * y_a + gb_ref[0].astype(F32) * y_b
    x = x_ref[0] + jnp.dot(merged.astype(BF16), wo_ref[...], preferred_element_type=F32)

    u = _rms(x, xg_ref[...]).astype(BF16)
    q = jnp.dot(u, wq_ref[...], preferred_element_type=F32).astype(BF16)
    heads = []
    for h in range(H_X):
        sl = slice(h * DH_X, (h + 1) * DH_X)
        s = lax.dot_general(q[:, sl], k_ref[0, :, sl], _NT, preferred_element_type=F32) * (DH_X ** -0.5)
        e = jnp.exp(s - jnp.max(s, axis=1, keepdims=True))
        p = e / jnp.sum(e, axis=1, keepdims=True)
        heads.append(jnp.dot(p.astype(BF16), v_ref[0, :, sl], preferred_element_type=F32))
    o = jnp.concatenate(heads, axis=1).astype(BF16)
    x = x + jnp.dot(o, wxo_ref[...], preferred_element_type=F32)

    u = _rms(x, fg_ref[...]).astype(BF16)
    z = jnp.dot(u, win_ref[...], preferred_element_type=F32)
    gate, up = z[:, :D_FF], z[:, D_FF:]
    hdn = (gate * _sigmoid(gate) * up).astype(BF16)
    x = x + 0.5 * jnp.dot(hdn, wout_ref[...], preferred_element_type=F32)
    o_ref[0] = _rms(x, fin_ref[...])


def _tail(x3d, hm, om, od, ga, gb, kx, vx, mix_out_w, xattn_w, ffn_w, fin):
    b, s, _ = x3d.shape
    m = kx.shape[1]
    tm = TOKEN_TILE
    gm, wa, wb, wo = mix_out_w
    xg, wq, wxo = xattn_w
    fg, win, wout = ffn_w
    row = pl.BlockSpec((1, tm, D_MODEL), lambda bi, i: (bi, i, 0))
    memb = pl.BlockSpec((1, m, D_MODEL), lambda bi, i: (bi, 0, 0))
    sq = _const_spec((D_MODEL, D_MODEL))
    vec = _const_spec((1, D_MODEL))
    return pl.pallas_call(
        _tail_kernel, grid=(b, s // tm),
        in_specs=[row, row, row, row, row, row, memb, memb, vec, sq, sq, sq, vec, sq, sq, vec,
                  _const_spec((D_MODEL, 2 * D_FF)), _const_spec((D_FF, D_MODEL)), vec],
        out_specs=row, out_shape=jax.ShapeDtypeStruct((b, s, D_MODEL), F32),
        compiler_params=_params("parallel", "parallel"), name="tail",
    )(x3d, hm, om, od, ga, gb, kx, vx, gm, wa, wb, wo, xg, wq, wxo, fg, win, wout, fin)


def _rope_tables(s):
    inv_freq = ROPE_THETA ** (-jnp.arange(0, ROT_DIM, 2, dtype=F32) / ROT_DIM)
    ang = jnp.arange(s, dtype=F32)[:, None] * inv_freq[None, :]
    half = ROT_DIM // 2
    one = jnp.ones((s, DK_D - ROT_DIM), F32)
    zero_h = jnp.zeros((s, half), F32)
    zero_r = jnp.zeros((s, DK_D - ROT_DIM), F32)
    cos = jnp.concatenate([jnp.cos(ang), jnp.cos(ang), one], axis=1)
    sa = jnp.concatenate([-jnp.sin(ang), zero_h, zero_r], axis=1)
    sb = jnp.concatenate([zero_h, jnp.sin(ang), zero_r], axis=1)
    return tuple(jnp.concatenate([t, t], axis=1) for t in (cos, sa, sb))


def _prep_weights(ffn1_norm, ffn1_w_in, ffn1_w_out, mix_norm, w_mix_in, b_igate, b_fgate, mlstm_norm, w_branch_a,
                  lambda_q1, lambda_k1, lambda_q2, lambda_k2, diff_norm, w_branch_b, w_mix_out, xattn_norm,
                  mem_norm, w_xq, w_xkv, w_xo, ffn2_norm, ffn2_w_in, ffn2_w_out, final_norm):
    vec = lambda a: a[0].reshape(1, -1).astype(F32)
    bf = lambda a: a.astype(BF16)
    offs = [0]
    for n in SPLIT_SIZES:
        offs.append(offs[-1] + n)
    cols = [w_mix_in[0][:, offs[i]:offs[i + 1]] for i in range(len(SPLIT_SIZES))]
    q_m, k_m, v_m, o_m, ig, fg, q_d, k_d, v_d, g_a, g_b = cols
    gate_cols, gate_bias = [], []
    for h in range(H_M):
        gate_cols += [ig[:, h], ig[:, H_M + h], fg[:, h], fg[:, H_M + h]]
        gate_bias += [b_igate[0, 0, h], b_igate[0, 1, h], b_fgate[0, 0, h], b_fgate[0, 1, h]]
    w_gt = jnp.stack(gate_cols, axis=0)
    b_g = jnp.stack(gate_bias).reshape(4 * H_M, 1).astype(F32)
    grp = (jnp.arange(2 * H_D * DK_D)[:, None] // DK_D == jnp.arange(V7X_LANES)[None, :]).astype(BF16)
    mix_w = (bf(q_m), bf(k_m), bf(v_m), bf(o_m), bf(q_d), bf(k_d), bf(v_d.T), bf(g_a), bf(g_b), bf(w_gt), b_g, grp)
    lam_params = jnp.concatenate([lambda_q1, lambda_k1, lambda_q2, lambda_k2], axis=0).astype(F32)
    return dict(
        ffn1=(vec(ffn1_norm), bf(ffn1_w_in[0]), bf(ffn1_w_out[0])),
        ffn2=(vec(ffn2_norm), bf(ffn2_w_in[0]), bf(ffn2_w_out[0])),
        final=final_norm.reshape(1, -1).astype(F32),
        mix_norm=vec(mix_norm), mix_w=mix_w, lam=lam_params,
        diff_gain=diff_norm[0].reshape(H_D, DV_D, 1).astype(F32),
        mix_out=(vec(mlstm_norm), bf(w_branch_a[0]), bf(w_branch_b[0]), bf(w_mix_out[0])),
        xattn=(vec(xattn_norm), vec(mem_norm), bf(w_xq[0]), bf(w_xkv[0]), bf(w_xo[0])),
    )


def _trunk(x, mem, w):
    b, s, d = x.shape
    t = b * s
    lambda_init = 0.8 - 0.6 * math.exp(-0.3 * 0)
    x1 = _ffn(x.reshape(t, d), *w["ffn1"])
    qm, km, vm, om, qd, kd, vdt, ga, gb, gates, key_sq = _mix_proj(x1.reshape(b, s, d), w["mix_norm"],
                                                                  _rope_tables(s), w["mix_w"])
    hm = _mlstm(qm, km, vm, gates)
    key_bound = jnp.sqrt(jnp.max(key_sq[:, :, 0, :2 * H_D], axis=1)).reshape(b, H_D, 2) * ATTN_BOUND_SLACK
    key_bound = jnp.broadcast_to(jnp.pad(key_bound, ((0, 0), (0, 0), (0, 6)))[..., None], (b, H_D, 8, V7X_LANES))
    od_fast, low = _diff_attn_fast(w["lam"], qd, kd, vdt, w["diff_gain"], key_bound, lambda_init)
    od = lax.cond(jnp.min(low) >= ATTN_MIN_ROW_SUM, lambda: od_fast,
                  lambda: _diff_attn(w["lam"], qd, kd, vdt, w["diff_gain"], lambda_init))
    xg, mg, wq, wkv, wo = w["xattn"]
    kx, vx = _mem_kv(mem, mg, wkv)
    return _tail(x1.reshape(b, s, d), hm, om, od, ga, gb, kx, vx, w["mix_out"], (xg, wq, wo), w["ffn2"], w["final"])


def kernel(x_prompt, x_sample, mem_prompt, mem_sample, ffn1_norm, ffn1_w_in, ffn1_w_out, mix_norm, w_mix_in, b_igate, b_fgate, mlstm_norm, w_branch_a, lambda_q1, lambda_k1, lambda_q2, lambda_k2, diff_norm, w_branch_b, w_mix_out, xattn_norm, mem_norm, w_xq, w_xkv, w_xo, ffn2_norm, ffn2_w_in, ffn2_w_out, final_norm):
    w = _prep_weights(ffn1_norm, ffn1_w_in, ffn1_w_out, mix_norm, w_mix_in, b_igate, b_fgate, mlstm_norm,
                      w_branch_a, lambda_q1, lambda_k1, lambda_q2, lambda_k2, diff_norm, w_branch_b, w_mix_out,
                      xattn_norm, mem_norm, w_xq, w_xkv, w_xo, ffn2_norm, ffn2_w_in, ffn2_w_out, final_norm)
    return (_trunk(x_prompt, mem_prompt, w), _trunk(x_sample, mem_sample, w))
```

```python
import functools
import math

import jax
import jax.numpy as jnp
from jax import lax
from jax.experimental import pallas as pl
from jax.experimental.pallas import tpu as pltpu

F32 = jnp.float32
BF16 = jnp.bfloat16

D_MODEL = 1024
EPS = 1e-6
LOG2_E = math.log2(math.e)
H_M, DK_M, DV_M = 4, 128, 256
W_M = H_M * DV_M
H_D, DK_D, DV_D = 8, 64, 128
W_D = H_D * DV_D
ROT_DIM = DK_D // 4
ROPE_THETA = 500000.0
H_X = 4
DH_X = D_MODEL // H_X
D_FF = 2816
SPLIT_SIZES = (H_M * DK_M, H_M * DK_M, W_M, W_M, 2 * H_M, 2 * H_M,
               H_D * 2 * DK_D, H_D * 2 * DK_D, W_D, D_MODEL, D_MODEL)

V7X_LANES = 128
V7X_VMEM_LIMIT_BYTES = 56 * 1024 * 1024

TOKEN_TILE = 256
FFN_TILE = 512
MLSTM_CHUNK = 256
ATTN_TQ = 1024
ATTN_TK = 512
ATTN_TILES_PER_STEP = 1
ATTN_BOUND_SLACK = 1.01
ATTN_MIN_ROW_SUM = 2.0 ** -40
ONES_ROWS = 16
VT_ROWS = DV_D + ONES_ROWS


def _params(*sem):
    return pltpu.CompilerParams(dimension_semantics=sem, vmem_limit_bytes=V7X_VMEM_LIMIT_BYTES)


def _const_spec(shape):
    n = len(shape)
    return pl.BlockSpec(shape, lambda *_: (0,) * n, pipeline_mode=pl.Buffered(1))


def _rms(x, g):
    return x * lax.rsqrt(jnp.mean(x * x, axis=-1, keepdims=True) + EPS) * g


def _sigmoid(x):
    return 1.0 / (1.0 + jnp.exp(-x))


_NT = (((1,), (1,)), ((), ()))
_TN = (((0,), (0,)), ((), ()))


def _ffn_kernel(x_ref, g_ref, win_ref, wout_ref, o_ref):
    x = x_ref[...]
    u = _rms(x, g_ref[...]).astype(BF16)
    z = jnp.dot(u, win_ref[...], preferred_element_type=F32)
    gate, up = z[:, :D_FF], z[:, D_FF:]
    h = (gate * _sigmoid(gate) * up).astype(BF16)
    o_ref[...] = x + 0.5 * jnp.dot(h, wout_ref[...], preferred_element_type=F32)


def _ffn(x2d, g, w_in, w_out):
    t = x2d.shape[0]
    tm = FFN_TILE
    row = pl.BlockSpec((tm, D_MODEL), lambda i: (i, 0))
    return pl.pallas_call(
        _ffn_kernel, grid=(t // tm,),
        in_specs=[row, _const_spec((1, D_MODEL)), _const_spec((D_MODEL, 2 * D_FF)), _const_spec((D_FF, D_MODEL))],
        out_specs=row, out_shape=jax.ShapeDtypeStruct((t, D_MODEL), F32),
        compiler_params=_params("parallel"), name="ffn",
    )(x2d, g, w_in, w_out)


def _mix_proj_kernel(x_ref, g_ref, cos_ref, sa_ref, sb_ref, wqm, wkm, wvm, wom, wqd, wkd, wvdt, wga, wgb,
                     wgt, bg_ref, grp_ref, qm_o, km_o, vm_o, om_o, qd_o, kd_o, vdt_o, ga_o, gb_o, gt_o, kn_o):
    u = _rms(x_ref[0], g_ref[...]).astype(BF16)
    tm = u.shape[0]

    xg = lax.dot_general(wgt[...], u, _NT, preferred_element_type=F32) + bg_ref[...]
    kind = lax.broadcasted_iota(jnp.int32, xg.shape, 0) % 4
    lane = lax.broadcasted_iota(jnp.int32, xg.shape, 1)
    log_sig = jnp.minimum(xg, 0.0) - jnp.log1p(jnp.exp(-jnp.abs(xg)))
    gates = jnp.where(kind < 2, xg, log_sig) * LOG2_E
    prefix = suffix = gates
    shift = 1
    while shift < tm:
        prefix = prefix + jnp.where(lane >= shift, pltpu.roll(prefix, shift, 1), 0.0)
        suffix = suffix + jnp.where(lane < tm - shift, pltpu.roll(suffix, tm - shift, 1), 0.0)
        shift *= 2
    gates = jnp.where(kind == 2, prefix, jnp.where(kind == 3, suffix, gates))
    for h in range(H_M):
        gt_o[0, h, 0] = gates[h * 4:(h + 1) * 4, :]

    def mm(w):
        return jnp.dot(u, w[...], preferred_element_type=F32)

    qm_o[0] = mm(wqm).astype(BF16)
    km_o[0] = (mm(wkm) * (DK_M ** -0.5)).astype(BF16)
    vm_o[0] = mm(wvm).astype(BF16)
    om_o[0] = _sigmoid(mm(wom)).astype(BF16)
    ga_o[0] = _sigmoid(mm(wga)).astype(BF16)
    gb_o[0] = _sigmoid(mm(wgb)).astype(BF16)

    cos, sa, sb = cos_ref[...], sa_ref[...], sb_ref[...]

    def rope(z):
        heads = []
        for h in range(H_D):
            zh = z[:, h * V7X_LANES:(h + 1) * V7X_LANES]
            heads.append(zh * cos + pltpu.roll(zh, V7X_LANES - ROT_DIM // 2, 1) * sa
                         + pltpu.roll(zh, ROT_DIM // 2, 1) * sb)
        return jnp.concatenate(heads, axis=1)

    qd_o[0] = (rope(mm(wqd)) * (DK_D ** -0.5 * LOG2_E)).astype(BF16)
    kf = rope(mm(wkd))
    kd_o[0] = kf.astype(BF16)
    norms = jnp.dot((kf * kf).astype(BF16), grp_ref[...], preferred_element_type=F32)
    kn_o[0, 0] = jnp.broadcast_to(jnp.max(norms, axis=0, keepdims=True), kn_o.shape[2:])

    vt = lax.dot_general(wvdt[...], u, _NT, preferred_element_type=F32)
    for h in range(H_D):
        vdt_o[0, h, 0, 0:DV_D, :] = vt[h * DV_D:(h + 1) * DV_D, :].astype(BF16)
        vdt_o[0, h, 0, DV_D:VT_ROWS, :] = jnp.ones((ONES_ROWS, tm), BF16)


def _mix_proj(x3d, g, tables, w):
    b, s, _ = x3d.shape
    tm = TOKEN_TILE
    assert tm == MLSTM_CHUNK and ATTN_TK % tm == 0
    per_tk = ATTN_TK // tm
    row = lambda n: pl.BlockSpec((1, tm, n), lambda bi, i: (bi, i, 0))
    tab = pl.BlockSpec((tm, V7X_LANES), lambda bi, i: (i, 0))
    bf = lambda n: jax.ShapeDtypeStruct((b, s, n), BF16)
    in_specs = [row(D_MODEL), _const_spec((1, D_MODEL)), tab, tab, tab,
                _const_spec((D_MODEL, H_M * DK_M)), _const_spec((D_MODEL, H_M * DK_M)),
                _const_spec((D_MODEL, W_M)), _const_spec((D_MODEL, W_M)),
                _const_spec((D_MODEL, 2 * H_D * DK_D)), _const_spec((D_MODEL, 2 * H_D * DK_D)),
                _const_spec((W_D, D_MODEL)), _const_spec((D_MODEL, D_MODEL)), _const_spec((D_MODEL, D_MODEL)),
                _const_spec((4 * H_M, D_MODEL)), _const_spec((4 * H_M, 1)),
                _const_spec((2 * H_D * DK_D, V7X_LANES))]
    out_specs = [row(H_M * DK_M), row(H_M * DK_M), row(W_M), row(W_M), row(2 * H_D * DK_D), row(2 * H_D * DK_D),
                 pl.BlockSpec((1, H_D, 1, VT_ROWS, tm), lambda bi, i: (bi, 0, i // per_tk, 0, i % per_tk)),
                 row(D_MODEL), row(D_MODEL),
                 pl.BlockSpec((1, H_M, 1, 4, tm), lambda bi, i: (bi, 0, i, 0, 0)),
                 pl.BlockSpec((1, 1, 8, V7X_LANES), lambda bi, i: (bi, i, 0, 0))]
    out_shape = [bf(H_M * DK_M), bf(H_M * DK_M), bf(W_M), bf(W_M), bf(2 * H_D * DK_D), bf(2 * H_D * DK_D),
                 jax.ShapeDtypeStruct((b, H_D, s // ATTN_TK, VT_ROWS, ATTN_TK), BF16),
                 bf(D_MODEL), bf(D_MODEL),
                 jax.ShapeDtypeStruct((b, H_M, s // tm, 4, tm), F32),
                 jax.ShapeDtypeStruct((b, s // tm, 8, V7X_LANES), F32)]
    return pl.pallas_call(
        _mix_proj_kernel, grid=(b, s // tm), in_specs=in_specs, out_specs=out_specs, out_shape=out_shape,
        compiler_params=_params("parallel", "parallel"), name="mix_proj",
    )(x3d, g, *tables, *w)


def _mlstm_chunk(q, k, v, ig, b, state, reverse):
    s_ext, m = state
    L, W = q.shape[0], V7X_LANES
    wide = lambda x, n: jnp.concatenate([x] * n, axis=1)
    col = lambda x: jnp.broadcast_to(x, (L, W))
    ri = lax.broadcasted_iota(jnp.int32, (L, L), 0)
    ci = lax.broadcasted_iota(jnp.int32, (L, L), 1)
    tri = (ci >= ri) if reverse else (ci <= ri)
    eye = ci == ri
    r_row = ig - b
    b_last = jnp.broadcast_to(b[:, 0:1] if reverse else b[:, L - 1:L], (1, W))
    r_col = col(jnp.sum(jnp.where(eye, r_row, 0.0), axis=1, keepdims=True))
    b_col = col(jnp.sum(jnp.where(eye, b, 0.0), axis=1, keepdims=True))

    r_tri = jnp.where(tri, r_row, -jnp.inf)
    t = jnp.maximum(m, col(jnp.max(r_tri, axis=1, keepdims=True)))
    qk = lax.dot_general(q, k, _NT, preferred_element_type=F32)
    w = jnp.exp2(r_tri - wide(t, L // W)) * qk
    s_inter = jnp.exp2(m - t)
    qs = jnp.dot(q, s_ext.astype(BF16), preferred_element_type=F32)
    num = wide(s_inter, DV_M // W) * qs[:, :DV_M] + jnp.dot(w.astype(BF16), v, preferred_element_type=F32)
    den = s_inter * qs[:, DV_M:] + col(jnp.sum(w, axis=1, keepdims=True))
    inv = 1.0 / jnp.maximum(jnp.abs(den), jnp.exp2(-(b_col + t)))
    h = num * wide(inv, DV_M // W)

    m_new = jnp.maximum(b_last + m, jnp.max(b_last[:, 0:1] + r_row, axis=1, keepdims=True))
    decay = jnp.exp2(b_last + m - m_new)
    kw = (k.astype(F32) * jnp.exp2(b_last + r_col - m_new)).astype(BF16)
    v_ext = jnp.concatenate([v, jnp.ones((L, W), BF16)], axis=1)
    s_new = wide(decay, DV_M // W + 1) * s_ext + lax.dot_general(kw, v_ext, _TN, preferred_element_type=F32)
    return h, (s_new, m_new)


def _mlstm_kernel(q_ref, k_ref, v_ref, g_ref, o_ref, *, n_chunks):
    L = MLSTM_CHUNK
    init = (jnp.zeros((DK_M, DV_M + V7X_LANES), F32), jnp.zeros((1, V7X_LANES), F32))

    def run(c, state, reverse):
        rows = pl.ds(pl.multiple_of(c * L, L), L)
        g = g_ref[0, 0, c]
        ig, b = (g[1:2], g[3:4]) if reverse else (g[0:1], g[2:3])
        return _mlstm_chunk(q_ref[0, rows, :], k_ref[0, rows, :], v_ref[0, rows, :], ig, b, state, reverse), rows

    def both(c, states, first_touch):
        (h_f, s_f), rows_f = run(c, states[0], False)
        (h_b, s_b), rows_b = run(n_chunks - 1 - c, states[1], True)
        if first_touch:
            o_ref[0, rows_f, :] = h_f
            o_ref[0, rows_b, :] = h_b
        else:
            o_ref[0, rows_f, :] += h_f
            o_ref[0, rows_b, :] += h_b
        return s_f, s_b

    half = n_chunks // 2
    states = lax.fori_loop(0, half, functools.partial(both, first_touch=True), (init, init))
    lax.fori_loop(half, n_chunks, functools.partial(both, first_touch=False), states)


def _mlstm(qm, km, vm, gates):
    b, s, _ = qm.shape
    n_chunks = s // MLSTM_CHUNK
    assert n_chunks % 2 == 0
    return pl.pallas_call(
        functools.partial(_mlstm_kernel, n_chunks=n_chunks),
        grid=(b, H_M),
        in_specs=[pl.BlockSpec((1, s, DK_M), lambda bi, h: (bi, 0, h)),
                  pl.BlockSpec((1, s, DK_M), lambda bi, h: (bi, 0, h)),
                  pl.BlockSpec((1, s, DV_M), lambda bi, h: (bi, 0, h)),
                  pl.BlockSpec((1, 1, n_chunks, 4, MLSTM_CHUNK), lambda bi, h: (bi, h, 0, 0, 0))],
        out_specs=pl.BlockSpec((1, s, DV_M), lambda bi, h: (bi, 0, h)),
        out_shape=jax.ShapeDtypeStruct((b, s, W_M), F32),
        compiler_params=_params("parallel", "parallel"), name="mlstm",
    )(qm, km, vm, gates)


def _diff_attn_kernel(lam_ref, q_ref, k_ref, vt_ref, g_ref, o_ref, *scratch, n_kv, lambda_init):
    acc_refs, stage_refs = scratch[:ATTN_TILES_PER_STEP], scratch[ATTN_TILES_PER_STEP:]
    for tile, acc_ref in enumerate(acc_refs):
        _diff_attn_tile(lam_ref, q_ref, k_ref, vt_ref, g_ref, o_ref, acc_ref, *stage_refs,
                        tile=tile, n_kv=n_kv, lambda_init=lambda_init)


def _diff_attn_tile(lam_ref, q_ref, k_ref, vt_ref, g_ref, o_ref, acc_ref, s0_ref, s1_ref, p0_ref, p1_ref,
                    *, tile, n_kv, lambda_init):
    tq, tk = ATTN_TQ, ATTN_TK
    s_bufs, p_bufs = (s0_ref, s1_ref), (p0_ref, p1_ref)
    qt = q_ref[0, tile * tq:(tile + 1) * tq, :].astype(F32).T.astype(BF16)
    row = lax.broadcasted_iota(jnp.int32, qt.shape, 0)
    zero = jnp.zeros_like(qt)
    q_bd = jnp.concatenate([jnp.where(row < DK_D, qt, zero), jnp.where(row >= DK_D, qt, zero)], axis=1)

    def scores(j, slot):
        ks = k_ref[0, j * tk:(j + 1) * tk, :]
        s = jnp.dot(ks, q_bd, preferred_element_type=F32)
        s_bufs[slot][...] = s
        return jnp.max(s, axis=0, keepdims=True)

    def softmax(slot, m, tile_max):
        m_new = jnp.maximum(m, tile_max)
        p_bufs[slot][...] = jnp.exp2(s_bufs[slot][...] - m_new).astype(BF16)
        return m_new, jnp.exp2(m - m_new)

    def accumulate(j, slot, alpha):
        pv = jnp.dot(vt_ref[0, 0, j], p_bufs[slot][...], preferred_element_type=F32)
        acc_ref[...] = acc_ref[...] * alpha + pv

    def step(j, slot, carry):
        m, tile_max, alpha_prev = carry
        next_max = scores(j + 1, 1 - slot)
        m, alpha = softmax(slot, m, tile_max)
        accumulate(j - 1, 1 - slot, alpha_prev)
        return m, next_max, alpha

    acc_ref[...] = jnp.zeros_like(acc_ref)
    tile_max = scores(0, 0)
    next_max = scores(1, 1)
    m, alpha = softmax(0, jnp.full((1, 2 * tq), -jnp.inf, F32), tile_max)

    carry = (m, next_max, alpha)
    for j in range(1, n_kv - 1):
        carry = step(j, j % 2, carry)
    m, tile_max, alpha_prev = carry
    last = n_kv - 1
    m, alpha = softmax(last % 2, m, tile_max)
    accumulate(last - 1, 1 - last % 2, alpha_prev)
    accumulate(last, last % 2, alpha)

    lp = lam_ref[...]
    lam = (jnp.exp(jnp.sum(lp[0:1] * lp[1:2], axis=1, keepdims=True))
           - jnp.exp(jnp.sum(lp[2:3] * lp[3:4], axis=1, keepdims=True)) + lambda_init)
    acc = acc_ref[...]
    row_sum = acc[DV_D:DV_D + 1, :]
    o = acc[:DV_D, :tq] / row_sum[:, :tq] - lam * (acc[:DV_D, tq:] / row_sum[:, tq:])
    o = o * lax.rsqrt(jnp.mean(o * o, axis=0, keepdims=True) + EPS) * g_ref[0]
    o_ref[0, tile * tq:(tile + 1) * tq, :] = (o * (1.0 - lambda_init)).T.astype(BF16)


def _diff_attn(lam_params, qd, kd, vdt, gain, lambda_init):
    b, s, _ = qd.shape
    n_kv = s // ATTN_TK
    assert n_kv >= 2 and n_kv % 2 == 0
    tq_step = ATTN_TQ * ATTN_TILES_PER_STEP
    return pl.pallas_call(
        functools.partial(_diff_attn_kernel, n_kv=n_kv, lambda_init=lambda_init),
        grid=(b, H_D, s // tq_step),
        in_specs=[_const_spec((4, DK_D)),
                  pl.BlockSpec((1, tq_step, 2 * DK_D), lambda bi, h, i: (bi, i, h)),
                  pl.BlockSpec((1, s, 2 * DK_D), lambda bi, h, i: (bi, 0, h)),
                  pl.BlockSpec((1, 1, n_kv, VT_ROWS, ATTN_TK), lambda bi, h, i: (bi, h, 0, 0, 0)),
                  pl.BlockSpec((1, DV_D, 1), lambda bi, h, i: (h, 0, 0))],
        out_specs=pl.BlockSpec((1, tq_step, DV_D), lambda bi, h, i: (bi, i, h)),
        out_shape=jax.ShapeDtypeStruct((b, s, W_D), BF16),
        scratch_shapes=[pltpu.VMEM((VT_ROWS, 2 * ATTN_TQ), F32)] * ATTN_TILES_PER_STEP + [
                        pltpu.VMEM((ATTN_TK, 2 * ATTN_TQ), F32), pltpu.VMEM((ATTN_TK, 2 * ATTN_TQ), F32),
                        pltpu.VMEM((ATTN_TK, 2 * ATTN_TQ), BF16), pltpu.VMEM((ATTN_TK, 2 * ATTN_TQ), BF16)],
        compiler_params=_params("parallel", "parallel", "arbitrary"), name="diff_attn",
    )(lam_params, qd, kd, vdt, gain)


def _diff_attn_fast_kernel(lam_ref, q_ref, k_ref, vt_ref, g_ref, kb_ref, o_ref, low_ref, *scratch,
                           n_kv, lambda_init):
    acc_refs, p_refs = scratch[:ATTN_TILES_PER_STEP], scratch[ATTN_TILES_PER_STEP:]
    lows = [_diff_attn_fast_tile(lam_ref, q_ref, k_ref, vt_ref, g_ref, kb_ref, o_ref, acc_ref, *p_refs,
                                 tile=tile, n_kv=n_kv, lambda_init=lambda_init)
            for tile, acc_ref in enumerate(acc_refs)]
    low = lows[0]
    for other in lows[1:]:
        low = jnp.minimum(low, other)
    low_ref[0, 0, 0] = jnp.broadcast_to(low, low_ref.shape[3:])


def _diff_attn_fast_tile(lam_ref, q_ref, k_ref, vt_ref, g_ref, kb_ref, o_ref, acc_ref, p0_ref, p1_ref,
                         *, tile, n_kv, lambda_init):
    tq, tk = ATTN_TQ, ATTN_TK
    p_bufs = (p0_ref, p1_ref)
    qf = q_ref[0, tile * tq:(tile + 1) * tq, :].astype(F32).T
    row = lax.broadcasted_iota(jnp.int32, qf.shape, 0)
    sq = qf * qf
    n1 = jnp.sqrt(jnp.sum(jnp.where(row < DK_D, sq, 0.0), axis=0, keepdims=True))
    n2 = jnp.sqrt(jnp.sum(jnp.where(row >= DK_D, sq, 0.0), axis=0, keepdims=True))
    kb = kb_ref[0, 0]
    wide = lambda x: jnp.concatenate([x] * (tq // V7X_LANES), axis=1)
    neg = -jnp.concatenate([n1 * wide(kb[0:1]), n2 * wide(kb[1:2])], axis=1)
    hi = neg.astype(BF16).astype(F32)
    mid = (neg - hi).astype(BF16).astype(F32)
    lo = (neg - hi - mid).astype(BF16).astype(F32)
    srow = lax.broadcasted_iota(jnp.int32, (V7X_LANES, 2 * tq), 0)
    shift = jnp.where(srow == 0, hi, jnp.where(srow == 1, mid, jnp.where(srow == 2, lo, 0.0))).astype(BF16)
    qt = qf.astype(BF16)
    zero = jnp.zeros_like(qt)
    q_bd = jnp.concatenate([jnp.where(row < DK_D, qt, zero), jnp.where(row >= DK_D, qt, zero)], axis=1)
    q_aug = jnp.concatenate([q_bd, shift], axis=0)
    ones = jnp.ones((tk, V7X_LANES), BF16)

    def scores(j, slot):
        ks = jnp.concatenate([k_ref[0, j * tk:(j + 1) * tk, :], ones], axis=1)
        s = jnp.dot(ks, q_aug, preferred_element_type=F32)
        p_bufs[slot][...] = jnp.exp2(s).astype(BF16)

    def accumulate(j, slot, first):
        pv = jnp.dot(vt_ref[0, 0, j], p_bufs[slot][...], preferred_element_type=F32)
        acc_ref[...] = pv if first else acc_ref[...] + pv

    scores(0, 0)
    for j in range(1, n_kv):
        scores(j, j % 2)
        accumulate(j - 1, (j - 1) % 2, j == 1)
    accumulate(n_kv - 1, (n_kv - 1) % 2, False)

    lp = lam_ref[...]
    lam = (jnp.exp(jnp.sum(lp[0:1] * lp[1:2], axis=1, keepdims=True))
           - jnp.exp(jnp.sum(lp[2:3] * lp[3:4], axis=1, keepdims=True)) + lambda_init)
    acc = acc_ref[...]
    row_sum = acc[DV_D:DV_D + 1, :]
    o = acc[:DV_D, :tq] / row_sum[:, :tq] - lam * (acc[:DV_D, tq:] / row_sum[:, tq:])
    o = o * lax.rsqrt(jnp.mean(o * o, axis=0, keepdims=True) + EPS) * g_ref[0]
    o_ref[0, tile * tq:(tile + 1) * tq, :] = (o * (1.0 - lambda_init)).T.astype(BF16)
    return jnp.min(row_sum, axis=1, keepdims=True)


def _diff_attn_fast(lam_params, qd, kd, vdt, gain, key_bound, lambda_init):
    b, s, _ = qd.shape
    n_kv = s // ATTN_TK
    assert n_kv >= 2
    tq_step = ATTN_TQ * ATTN_TILES_PER_STEP
    n_steps = s // tq_step
    return pl.pallas_call(
        functools.partial(_diff_attn_fast_kernel, n_kv=n_kv, lambda_init=lambda_init),
        grid=(b, H_D, n_steps),
        in_specs=[_const_spec((4, DK_D)),
                  pl.BlockSpec((1, tq_step, 2 * DK_D), lambda bi, h, i: (bi, i, h)),
                  pl.BlockSpec((1, s, 2 * DK_D), lambda bi, h, i: (bi, 0, h)),
                  pl.BlockSpec((1, 1, n_kv, VT_ROWS, ATTN_TK), lambda bi, h, i: (bi, h, 0, 0, 0)),
                  pl.BlockSpec((1, DV_D, 1), lambda bi, h, i: (h, 0, 0)),
                  pl.BlockSpec((1, 1, 8, V7X_LANES), lambda bi, h, i: (bi, h, 0, 0))],
        out_specs=[pl.BlockSpec((1, tq_step, DV_D), lambda bi, h, i: (bi, i, h)),
                   pl.BlockSpec((1, 1, 1, 8, V7X_LANES), lambda bi, h, i: (bi, h, i, 0, 0))],
        out_shape=[jax.ShapeDtypeStruct((b, s, W_D), BF16),
                   jax.ShapeDtypeStruct((b, H_D, n_steps, 8, V7X_LANES), F32)],
        scratch_shapes=[pltpu.VMEM((VT_ROWS, 2 * ATTN_TQ), F32)] * ATTN_TILES_PER_STEP + [
                        pltpu.VMEM((ATTN_TK, 2 * ATTN_TQ), BF16), pltpu.VMEM((ATTN_TK, 2 * ATTN_TQ), BF16)],
        compiler_params=_params("parallel", "parallel", "arbitrary"), name="diff_attn_fast",
    )(lam_params, qd, kd, vdt, gain, key_bound)


def _mem_kv_kernel(mem_ref, g_ref, wkv_ref, k_o, v_o):
    mn = _rms(mem_ref[0], g_ref[...]).astype(BF16)
    kv = jnp.dot(mn, wkv_ref[...], preferred_element_type=F32)
    k_o[0] = kv[:, :D_MODEL].astype(BF16)
    v_o[0] = kv[:, D_MODEL:].astype(BF16)


def _mem_kv(mem, g, wkv):
    b, m, _ = mem.shape
    blk = pl.BlockSpec((1, m, D_MODEL), lambda bi: (bi, 0, 0))
    return pl.pallas_call(
        _mem_kv_kernel, grid=(b,),
        in_specs=[blk, _const_spec((1, D_MODEL)), _const_spec((D_MODEL, 2 * D_MODEL))],
        out_specs=[blk, blk], out_shape=[jax.ShapeDtypeStruct((b, m, D_MODEL), BF16)] * 2,
        compiler_params=_params("parallel"), name="mem_kv",
    )(mem, g, wkv)


def _tail_kernel(x_ref, hm_ref, om_ref, od_ref, ga_ref, gb_ref, k_ref, v_ref, gm_ref, wa_ref, wb_ref, wo_ref,
                 xg_ref, wq_ref, wxo_ref, fg_ref, win_ref, wout_ref, fin_ref, o_ref):
    hm = hm_ref[0]
    gm = gm_ref[...]
    heads = []
    for h in range(H_M):
        sl = slice(h * DV_M, (h + 1) * DV_M)
        heads.append(_rms(hm[:, sl], gm[:, sl]))
    hn = jnp.concatenate(heads, axis=1) * om_ref[0].astype(F32)
    y_a = jnp.dot(hn.astype(BF16), wa_ref[...], preferred_element_type=F32)
    y_b = jnp.dot(od_ref[0], wb_ref[...], preferred_element_type=F32)
    merged = ga_ref[0].astype(F32) * y_a + gb_ref[0].astype(F32) * y_b
    x = x_ref[0] + jnp.dot(merged.astype(BF16), wo_ref[...], preferred_element_type=F32)

    u = _rms(x, xg_ref[...]).astype(BF16)
    q = jnp.dot(u, wq_ref[...], preferred_element_type=F32).astype(BF16)
    heads = []
    for h in range(H_X):
        sl = slice(h * DH_X, (h + 1) * DH_X)
        s = lax.dot_general(q[:, sl], k_ref[0, :, sl], _NT, preferred_element_type=F32) * (DH_X ** -0.5)
        e = jnp.exp(s - jnp.max(s, axis=1, keepdims=True))
        p = e / jnp.sum(e, axis=1, keepdims=True)
        heads.append(jnp.dot(p.astype(BF16), v_ref[0, :, sl], preferred_element_type=F32))
    o = jnp.concatenate(heads, axis=1).astype(BF16)
    x = x + jnp.dot(o, wxo_ref[...], preferred_element_type=F32)

    u = _rms(x, fg_ref[...]).astype(BF16)
    z = jnp.dot(u, win_ref[...], preferred_element_type=F32)
    gate, up = z[:, :D_FF], z[:, D_FF:]
    hdn = (gate * _sigmoid(gate) * up).astype(BF16)
    x = x + 0.5 * jnp.dot(hdn, wout_ref[...], preferred_element_type=F32)
    o_ref[0] = _rms(x, fin_ref[...])


def _tail(x3d, hm, om, od, ga, gb, kx, vx, mix_out_w, xattn_w, ffn_w, fin):
    b, s, _ = x3d.shape
    m = kx.shape[1]
    tm = TOKEN_TILE
    gm, wa, wb, wo = mix_out_w
    xg, wq, wxo = xattn_w
    fg, win, wout = ffn_w
    row = pl.BlockSpec((1, tm, D_MODEL), lambda bi, i: (bi, i, 0))
    memb = pl.BlockSpec((1, m, D_MODEL), lambda bi, i: (bi, 0, 0))
    sq = _const_spec((D_MODEL, D_MODEL))
    vec = _const_spec((1, D_MODEL))
    return pl.pallas_call(
        _tail_kernel, grid=(b, s // tm),
        in_specs=[row, row, row, row, row, row, memb, memb, vec, sq, sq, sq, vec, sq, sq, vec,
                  _const_spec((D_MODEL, 2 * D_FF)), _const_spec((D_FF, D_MODEL)), vec],
        out_specs=row, out_shape=jax.ShapeDtypeStruct((b, s, D_MODEL), F32),
        compiler_params=_params("parallel", "parallel"), name="tail",
    )(x3d, hm, om, od, ga, gb, kx, vx, gm, wa, wb, wo, xg, wq, wxo, fg, win, wout, fin)


def _rope_tables(s):
    inv_freq = ROPE_THETA ** (-jnp.arange(0, ROT_DIM, 2, dtype=F32) / ROT_DIM)
    ang = jnp.arange(s, dtype=F32)[:, None] * inv_freq[None, :]
    half = ROT_DIM // 2
    one = jnp.ones((s, DK_D - ROT_DIM), F32)
    zero_h = jnp.zeros((s, half), F32)
    zero_r = jnp.zeros((s, DK_D - ROT_DIM), F32)
    cos = jnp.concatenate([jnp.cos(ang), jnp.cos(ang), one], axis=1)
    sa = jnp.concatenate([-jnp.sin(ang), zero_h, zero_r], axis=1)
    sb = jnp.concatenate([zero_h, jnp.sin(ang), zero_r], axis=1)
    return tuple(jnp.concatenate([t, t], axis=1) for t in (cos, sa, sb))


def _prep_weights(ffn1_norm, ffn1_w_in, ffn1_w_out, mix_norm, w_mix_in, b_igate, b_fgate, mlstm_norm, w_branch_a,
                  lambda_q1, lambda_k1, lambda_q2, lambda_k2, diff_norm, w_branch_b, w_mix_out, xattn_norm,
                  mem_norm, w_xq, w_xkv, w_xo, ffn2_norm, ffn2_w_in, ffn2_w_out, final_norm):
    vec = lambda a: a[0].reshape(1, -1).astype(F32)
    bf = lambda a: a.astype(BF16)
    offs = [0]
    for n in SPLIT_SIZES:
        offs.append(offs[-1] + n)
    cols = [w_mix_in[0][:, offs[i]:offs[i + 1]] for i in range(len(SPLIT_SIZES))]
    q_m, k_m, v_m, o_m, ig, fg, q_d, k_d, v_d, g_a, g_b = cols
    gate_cols, gate_bias = [], []
    for h in range(H_M):
        gate_cols += [ig[:, h], ig[:, H_M + h], fg[:, h], fg[:, H_M + h]]
        gate_bias += [b_igate[0, 0, h], b_igate[0, 1, h], b_fgate[0, 0, h], b_fgate[0, 1, h]]
    w_gt = jnp.stack(gate_cols, axis=0)
    b_g = jnp.stack(gate_bias).reshape(4 * H_M, 1).astype(F32)
    grp = (jnp.arange(2 * H_D * DK_D)[:, None] // DK_D == jnp.arange(V7X_LANES)[None, :]).astype(BF16)
    mix_w = (bf(q_m), bf(k_m), bf(v_m), bf(o_m), bf(q_d), bf(k_d), bf(v_d.T), bf(g_a), bf(g_b), bf(w_gt), b_g, grp)
    lam_params = jnp.concatenate([lambda_q1, lambda_k1, lambda_q2, lambda_k2], axis=0).astype(F32)
    return dict(
        ffn1=(vec(ffn1_norm), bf(ffn1_w_in[0]), bf(ffn1_w_out[0])),
        ffn2=(vec(ffn2_norm), bf(ffn2_w_in[0]), bf(ffn2_w_out[0])),
        final=final_norm.reshape(1, -1).astype(F32),
        mix_norm=vec(mix_norm), mix_w=mix_w, lam=lam_params,
        diff_gain=diff_norm[0].reshape(H_D, DV_D, 1).astype(F32),
        mix_out=(vec(mlstm_norm), bf(w_branch_a[0]), bf(w_branch_b[0]), bf(w_mix_out[0])),
        xattn=(vec(xattn_norm), vec(mem_norm), bf(w_xq[0]), bf(w_xkv[0]), bf(w_xo[0])),
    )


def _trunk(x, mem, w):
    b, s, d = x.shape
    t = b * s
    lambda_init = 0.8 - 0.6 * math.exp(-0.3 * 0)
    x1 = _ffn(x.reshape(t, d), *w["ffn1"])
    qm, km, vm, om, qd, kd, vdt, ga, gb, gates, key_sq = _mix_proj(x1.reshape(b, s, d), w["mix_norm"],
                                                                  _rope_tables(s), w["mix_w"])
    hm = _mlstm(qm, km, vm, gates)
    key_bound = jnp.sqrt(jnp.max(key_sq[:, :, 0, :2 * H_D], axis=1)).reshape(b, H_D, 2) * ATTN_BOUND_SLACK
    key_bound = jnp.broadcast_to(jnp.pad(key_bound, ((0, 0), (0, 0), (0, 6)))[..., None], (b, H_D, 8, V7X_LANES))
    od_fast, low = _diff_attn_fast(w["lam"], qd, kd, vdt, w["diff_gain"], key_bound, lambda_init)
    od = lax.cond(jnp.min(low) >= ATTN_MIN_ROW_SUM, lambda: od_fast,
                  lambda: _diff_attn(w["lam"], qd, kd, vdt, w["diff_gain"], lambda_init))
    xg, mg, wq, wkv, wo = w["xattn"]
    kx, vx = _mem_kv(mem, mg, wkv)
    return _tail(x1.reshape(b, s, d), hm, om, od, ga, gb, kx, vx, w["mix_out"], (xg, wq, wo), w["ffn2"], w["final"])


def kernel(x_prompt, x_sample, mem_prompt, mem_sample, ffn1_norm, ffn1_w_in, ffn1_w_out, mix_norm, w_mix_in, b_igate, b_fgate, mlstm_norm, w_branch_a, lambda_q1, lambda_k1, lambda_q2, lambda_k2, diff_norm, w_branch_b, w_mix_out, xattn_norm, mem_norm, w_xq, w_xkv, w_xo, ffn2_norm, ffn2_w_in, ffn2_w_out, final_norm):
    w = _prep_weights(ffn1_norm, ffn1_w_in, ffn1_w_out, mix_norm, w_mix_in, b_igate, b_fgate, mlstm_norm,
                      w_branch_a, lambda_q1, lambda_k1, lambda_q2, lambda_k2, diff_norm, w_branch_b, w_mix_out,
                      xattn_norm, mem_norm, w_xq, w_xkv, w_xo, ffn2_norm, ffn2_w_in, ffn2_w_out, final_norm)
    return (_trunk(x_prompt, mem_prompt, w), _trunk(x_sample, mem_sample, w))
```
